```python
import jax, jax.numpy as jnp
from jax import lax
import numpy as np

D_MODEL = 2048
BATCH = 4
SEQ = 4096
DEPTH = 1

C_CONV = D_MODEL // 2
C_POOL = D_MODEL // 2
CONV_WIDTH = 31
POOL_WINDOWS = (2, 4, 8, 16)
N_POOL_GROUPS = len(POOL_WINDOWS)
POOL_GROUP_DIM = C_POOL // N_POOL_GROUPS
W_IN_COLS = 2 * C_CONV + C_POOL + 2 * D_MODEL
N_EXPERTS = 32
TOP_K = 4
D_EXPERT = D_MODEL
SWIGLU_ALPHA = 1.702
SWIGLU_LIMIT = 7.0
NORM_EPS = 1e-5

kernel_name = "hybrid_conv_pool_gated_moe_block"


def rmsnorm(x, g):
    x32 = x.astype(jnp.float32)
    y = x32 * lax.rsqrt(jnp.mean(x32 * x32, axis=-1, keepdims=True) + NORM_EPS)
    return (y * g.astype(jnp.float32)).astype(x.dtype)


def layernorm(x, g, b):
    x32 = x.astype(jnp.float32)
    mu = jnp.mean(x32, axis=-1, keepdims=True)
    var = jnp.mean(jnp.square(x32 - mu), axis=-1, keepdims=True)
    y = (x32 - mu) * lax.rsqrt(var + NORM_EPS)
    return (y * g.astype(jnp.float32) + b.astype(jnp.float32)).astype(x.dtype)


def causal_depthwise_conv(u, w, b):
    y = lax.conv_general_dilated(
        u, w[:, None, :].astype(u.dtype), window_strides=(1,),
        padding=[(CONV_WIDTH - 1, 0)],
        dimension_numbers=("NWC", "WIO", "NWC"),
        feature_group_count=u.shape[-1])
    return y + b


def conformer_conv_branch(a, gate_in, conv_w, conv_b, ln_g, ln_b):
    u = a * jax.nn.sigmoid(gate_in)
    u = causal_depthwise_conv(u, conv_w, conv_b)
    u = layernorm(u, ln_g, ln_b)
    return jax.nn.silu(u)


def multiscale_pool_branch(p, pool_w, pool_scale):
    B, S, _ = p.shape
    pg = p.reshape(B, S, N_POOL_GROUPS, POOL_GROUP_DIM)
    cs = jnp.cumsum(pg.astype(jnp.float32), axis=1)
    pos = jnp.arange(1, S + 1, dtype=jnp.float32)[None, :, None]
    means = []
    for gi, w in enumerate(POOL_WINDOWS):
        c = cs[:, :, gi]
        lo = jnp.pad(c[:, :S - w], ((0, 0), (w, 0), (0, 0)))
        means.append((c - lo) / jnp.minimum(pos, float(w)))
    pooled = jnp.stack(means, axis=2)
    diff = (pooled - pg.astype(jnp.float32)).astype(p.dtype)
    y = jnp.einsum("bsgc,gcd->bsgd", diff, pool_w).reshape(B, S, C_POOL)
    return y * pool_scale


def gated_mixer(xn, w_in, conv_w, conv_b, conv_ln_g, conv_ln_b, w_conv_out,
                pool_w, pool_scale, w_pool_out, w_o):
    h = xn @ w_in
    o1 = C_CONV
    o2 = o1 + C_CONV
    o3 = o2 + C_POOL
    o4 = o3 + D_MODEL
    a, a_gate, p = h[..., :o1], h[..., o1:o2], h[..., o2:o3]
    g_conv, g_pool = h[..., o3:o4], h[..., o4:]
    y_conv = conformer_conv_branch(a, a_gate, conv_w, conv_b, conv_ln_g, conv_ln_b) @ w_conv_out
    y_pool = multiscale_pool_branch(p, pool_w, pool_scale) @ w_pool_out
    merged = jax.nn.sigmoid(g_conv) * y_conv + jax.nn.sigmoid(g_pool) * y_pool
    return merged @ w_o


def clamped_swiglu(h):
    x_glu, x_lin = jnp.split(h, 2, axis=-1)
    x_glu = jnp.minimum(x_glu, SWIGLU_LIMIT)
    x_lin = jnp.clip(x_lin, -SWIGLU_LIMIT, SWIGLU_LIMIT)
    return x_glu * jax.nn.sigmoid(SWIGLU_ALPHA * x_glu) * (x_lin + 1.0)


def moe_ffn(xn, w_router, b_router, w1, b1, w2, b2):
    B, S, D = xn.shape
    t = xn.reshape(B * S, D)
    logits = (t @ w_router + b_router).astype(jnp.float32)
    top_val, top_idx = lax.top_k(logits, TOP_K)
    top_w = jax.nn.softmax(top_val, axis=-1)
    gates = jnp.einsum("tk,tke->te", top_w,
                       jax.nn.one_hot(top_idx, N_EXPERTS, dtype=jnp.float32)).astype(t.dtype)
    out = jnp.zeros_like(t)
    for e in range(N_EXPERTS):
        act = clamped_swiglu(t @ w1[e] + b1[e])
        out = out + gates[:, e:e + 1] * (act @ w2[e] + b2[e])
    return out.reshape(B, S, D)


def setup_inputs(seed: int = 0) -> dict:
    key = jax.random.key(seed)
    ks = jax.random.split(key, 24)
    f32 = jnp.float32
    nrm = lambda k, shape, scale: jax.random.normal(k, shape, f32) * scale
    L = DEPTH
    return {
        "x": jax.random.normal(ks[0], (BATCH, SEQ, D_MODEL), f32),
        "norm_mix_g": 1.0 + nrm(ks[1], (L, D_MODEL), 0.02),
        "w_in": nrm(ks[2], (L, D_MODEL, W_IN_COLS), D_MODEL ** -0.5),
        "conv_w": nrm(ks[3], (L, CONV_WIDTH, C_CONV), CONV_WIDTH ** -0.5),
        "conv_b": nrm(ks[4], (L, C_CONV), 0.01),
        "conv_ln_g": 1.0 + nrm(ks[5], (L, C_CONV), 0.02),
        "conv_ln_b": nrm(ks[6], (L, C_CONV), 0.01),
        "w_conv_out": nrm(ks[7], (L, C_CONV, D_MODEL), C_CONV ** -0.5),
        "pool_w": nrm(ks[8], (L, N_POOL_GROUPS, POOL_GROUP_DIM, POOL_GROUP_DIM), POOL_GROUP_DIM ** -0.5),
        "pool_scale": 1.0 + nrm(ks[9], (L, C_POOL), 0.1),
        "w_pool_out": nrm(ks[10], (L, C_POOL, D_MODEL), C_POOL ** -0.5),
        "w_o": nrm(ks[11], (L, D_MODEL, D_MODEL), D_MODEL ** -0.5),
        "norm_ffn_g": 1.0 + nrm(ks[12], (L, D_MODEL), 0.02),
        "w_router": nrm(ks[13], (L, D_MODEL, N_EXPERTS), D_MODEL ** -0.5),
        "b_router": nrm(ks[14], (L, N_EXPERTS), 0.01),
        "w1": nrm(ks[15], (L, N_EXPERTS, D_MODEL, 2 * D_EXPERT), D_MODEL ** -0.5),
        "b1": nrm(ks[16], (L, N_EXPERTS, 2 * D_EXPERT), 0.01),
        "w2": nrm(ks[17], (L, N_EXPERTS, D_EXPERT, D_MODEL), D_EXPERT ** -0.5),
        "b2": nrm(ks[18], (L, N_EXPERTS, D_MODEL), 0.01),
        "norm_final_g": 1.0 + nrm(ks[19], (D_MODEL,), 0.02),
    }


def reference(x, norm_mix_g, w_in, conv_w, conv_b, conv_ln_g, conv_ln_b, w_conv_out,
              pool_w, pool_scale, w_pool_out, w_o, norm_ffn_g, w_router, b_router,
              w1, b1, w2, b2, norm_final_g):
    h = x
    for l in range(DEPTH):
        xn = rmsnorm(h, norm_mix_g[l])
        h = h + gated_mixer(xn, w_in[l], conv_w[l], conv_b[l], conv_ln_g[l], conv_ln_b[l],
                            w_conv_out[l], pool_w[l], pool_scale[l], w_pool_out[l], w_o[l])
        xn = rmsnorm(h, norm_ffn_g[l])
        h = h + moe_ffn(xn, w_router[l], b_router[l], w1[l], b1[l], w2[l], b2[l])
    return rmsnorm(h, norm_final_g)
```

```python
import functools

import jax
import jax.numpy as jnp
from jax import lax
from jax.experimental import pallas as pl
from jax.experimental.pallas import tpu as pltpu

NORM_EPS = 1e-5
POOL_WINDOWS = (2, 4, 8, 16)
TOP_K = 4
SWIGLU_ALPHA = 1.702
SWIGLU_LIMIT = 7.0

V7X_LANES = 128
V7X_SUBLANES = 8
V7X_VMEM_BYTES = 64 * 1024 * 1024
VMEM_LIMIT_BYTES = V7X_VMEM_BYTES - 8 * 1024 * 1024

F32 = jnp.float32
BF16 = jnp.bfloat16
I32 = jnp.int32
HI_MASK = -65536


def _cparams(*sem):
    return pltpu.CompilerParams(dimension_semantics=sem, vmem_limit_bytes=VMEM_LIMIT_BYTES)


def _tile(n, pref):
    t = min(n, pref)
    while n % t:
        t //= 2
    return t


def _rms(x, g):
    ms = jnp.mean(x * x, axis=-1, keepdims=True)
    return x * lax.rsqrt(ms + NORM_EPS) * g


def _inproj_prologue(x_ref, g_ref, xn_ref):
    @pl.when(pl.program_id(1) == 0)
    def _():
        xn_ref[...] = _rms(x_ref[...], g_ref[...]).astype(BF16)


def _inproj_glu_kernel(x_ref, g_ref, wa_ref, wb_ref, o_ref, xn_ref):
    _inproj_prologue(x_ref, g_ref, xn_ref)
    xn = xn_ref[...]
    a = jnp.dot(xn, wa_ref[...], preferred_element_type=F32)
    b = jnp.dot(xn, wb_ref[...], preferred_element_type=F32)
    o_ref[...] = (a * jax.nn.sigmoid(b)).astype(o_ref.dtype)


def _inproj_lin_kernel(x_ref, g_ref, w_ref, o_ref, xn_ref):
    _inproj_prologue(x_ref, g_ref, xn_ref)
    o_ref[...] = jnp.dot(xn_ref[...], w_ref[...], preferred_element_type=F32).astype(o_ref.dtype)


def _inproj_sig_kernel(x_ref, g_ref, w_ref, o_ref, xn_ref):
    _inproj_prologue(x_ref, g_ref, xn_ref)
    h = jnp.dot(xn_ref[...], w_ref[...], preferred_element_type=F32)
    o_ref[...] = jax.nn.sigmoid(h).astype(o_ref.dtype)


def _inproj(x2, g, w_bf, col0, ncols, mode, out_dtype, tm, tn):
    T, D = x2.shape
    nb = ncols // tn
    c0 = col0 // tn
    x_spec = pl.BlockSpec((tm, D), lambda i, j: (i, 0))
    g_spec = pl.BlockSpec((1, D), lambda i, j: (0, 0))
    w_spec = pl.BlockSpec((D, tn), lambda i, j: (0, c0 + j))
    if mode == "glu":
        kern = _inproj_glu_kernel
        in_specs = [x_spec, g_spec, w_spec, pl.BlockSpec((D, tn), lambda i, j: (0, c0 + nb + j))]
        args = (x2, g, w_bf, w_bf)
    else:
        kern = _inproj_lin_kernel if mode == "lin" else _inproj_sig_kernel
        in_specs = [x_spec, g_spec, w_spec]
        args = (x2, g, w_bf)
    return pl.pallas_call(
        kern,
        out_shape=jax.ShapeDtypeStruct((T, ncols), out_dtype),
        grid=(T // tm, nb),
        in_specs=in_specs,
        out_specs=pl.BlockSpec((tm, tn), lambda i, j: (i, j)),
        scratch_shapes=[pltpu.VMEM((tm, D), BF16)],
        compiler_params=_cparams("parallel", "arbitrary"),
        name="inproj_" + mode,
    )(*args)


CONV_HALO = 32
CONV_ROWS = 16
CONV_LANES = 512


def _conv_kernel(cur_ref, halo_ref, w_ref, b_ref, g_ref, beta_ref, o_ref, ext_ref, *, ts, C, KW):
    s = pl.program_id(1)
    halo = halo_ref[...].astype(F32)
    ext_ref[0:CONV_HALO, :] = jnp.where(s > 0, halo, 0.0)
    ext_ref[CONV_HALO:, :] = cur_ref[...].astype(F32)
    off = CONV_HALO - (KW - 1)
    lc = min(C, CONV_LANES)

    def body(i, carry):
        r0 = pl.multiple_of(i * CONV_ROWS, CONV_ROWS)
        parts = []
        for c0 in range(0, C, lc):
            win = ext_ref[pl.ds(r0, CONV_ROWS + CONV_HALO), c0:c0 + lc]
            acc = jnp.zeros((CONV_ROWS, lc), F32) + b_ref[:, c0:c0 + lc]
            for k in range(KW):
                acc = acc + win[off + k:off + k + CONV_ROWS, :] * w_ref[k:k + 1, c0:c0 + lc]
            parts.append(acc)
        y = jnp.concatenate(parts, axis=1) if len(parts) > 1 else parts[0]
        mu = jnp.mean(y, axis=-1, keepdims=True)
        d = y - mu
        var = jnp.mean(d * d, axis=-1, keepdims=True)
        yn = d * lax.rsqrt(var + NORM_EPS) * g_ref[...] + beta_ref[...]
        o_ref[pl.ds(r0, CONV_ROWS), :] = (yn * jax.nn.sigmoid(yn)).astype(o_ref.dtype)
        return carry

    lax.fori_loop(0, ts // CONV_ROWS, body, 0)


def _conv_branch(u3, conv_w, conv_b, ln_g, ln_b, ts):
    B, S, C = u3.shape
    KW = conv_w.shape[0]
    assert KW - 1 <= CONV_HALO and ts % CONV_HALO == 0
    hb = ts // CONV_HALO
    kern = functools.partial(_conv_kernel, ts=ts, C=C, KW=KW)
    vec = lambda: pl.BlockSpec((1, C), lambda b, s: (0, 0))
    return pl.pallas_call(
        kern,
        out_shape=jax.ShapeDtypeStruct((B, S, C), BF16),
        grid=(B, S // ts),
        in_specs=[
            pl.BlockSpec((None, ts, C), lambda b, s: (b, s, 0)),
            pl.BlockSpec((None, CONV_HALO, C), lambda b, s: (b, jnp.maximum(s * hb - 1, 0), 0)),
            pl.BlockSpec((KW, C), lambda b, s: (0, 0)),
            vec(), vec(), vec(),
        ],
        out_specs=pl.BlockSpec((None, ts, C), lambda b, s: (b, s, 0)),
        scratch_shapes=[pltpu.VMEM((CONV_HALO + ts, C), F32)],
        compiler_params=_cparams("parallel", "arbitrary"),
        name="conv_branch",
    )(u3, u3, conv_w, conv_b.reshape(1, C), ln_g.reshape(1, C), ln_b.reshape(1, C))


POOL_HALO = 16
POOL_ROWS = 32


def _pool_kernel(cur_ref, halo_ref, pw_ref, ps_ref, o_ref, ext_ref, diff_ref, *, ts, C):
    s = pl.program_id(1)
    G = len(POOL_WINDOWS)
    cg = C // G
    ext_ref[0:POOL_HALO, :] = jnp.where(s > 0, halo_ref[...], 0.0)
    ext_ref[POOL_HALO:, :] = cur_ref[...]

    def body(i, carry):
        r0 = pl.multiple_of(i * POOL_ROWS, POOL_ROWS)
        row = lax.broadcasted_iota(I32, (POOL_ROWS, 1), 0)
        pos1 = (s * ts + r0 + row + 1).astype(F32)
        for gi, w in enumerate(POOL_WINDOWS):
            win = ext_ref[pl.ds(r0, POOL_ROWS + POOL_HALO), gi * cg:(gi + 1) * cg]
            tok = win[POOL_HALO:POOL_HALO + POOL_ROWS, :]
            tot = tok
            for j in range(1, w):
                tot = tot + win[POOL_HALO - j:POOL_HALO - j + POOL_ROWS, :]
            d = tot / jnp.minimum(pos1, float(w)) - tok
            diff_ref[pl.ds(r0, POOL_ROWS), gi * cg:(gi + 1) * cg] = d.astype(BF16)
        return carry

    lax.fori_loop(0, ts // POOL_ROWS, body, 0)
    for gi in range(G):
        y = jnp.dot(diff_ref[:, gi * cg:(gi + 1) * cg], pw_ref[gi], preferred_element_type=F32)
        o_ref[:, gi * cg:(gi + 1) * cg] = (y * ps_ref[:, gi * cg:(gi + 1) * cg]).astype(o_ref.dtype)


def _pool_branch(p3, pool_w_bf, pool_scale, ts):
    B, S, C = p3.shape
    G, cg, _ = pool_w_bf.shape
    assert G == len(POOL_WINDOWS) and max(POOL_WINDOWS) - 1 <= POOL_HALO and ts % POOL_ROWS == 0
    hb = ts // POOL_HALO
    kern = functools.partial(_pool_kernel, ts=ts, C=C)
    return pl.pallas_call(
        kern,
        out_shape=jax.ShapeDtypeStruct((B, S, C), BF16),
        grid=(B, S // ts),
        in_specs=[
            pl.BlockSpec((None, ts, C), lambda b, s: (b, s, 0)),
            pl.BlockSpec((None, POOL_HALO, C), lambda b, s: (b, jnp.maximum(s * hb - 1, 0), 0)),
            pl.BlockSpec((G, cg, cg), lambda b, s: (0, 0, 0)),
            pl.BlockSpec((1, C), lambda b, s: (0, 0)),
        ],
        out_specs=pl.BlockSpec((None, ts, C), lambda b, s: (b, s, 0)),
        scratch_shapes=[pltpu.VMEM((POOL_HALO + ts, C), F32), pltpu.VMEM((ts, C), BF16)],
        compiler_params=_cparams("parallel", "arbitrary"),
        name="pool_branch",
    )(p3, p3, pool_w_bf, pool_scale.reshape(1, C))


def _pack_bf16_pairs(x):
    half = x.shape[1] // 2
    xb = x.astype(BF16).astype(F32)
    lo = lax.shift_right_logical(lax.bitcast_convert_type(xb[:, :half], I32), 16)
    hi = lax.bitcast_convert_type(xb[:, half:], I32) & HI_MASK
    return lo | hi


def _unpack_bf16_pairs(xp):
    lo = lax.bitcast_convert_type(lax.shift_left(xp, 16), F32).astype(BF16)
    hi = lax.bitcast_convert_type(xp & HI_MASK, F32).astype(BF16)
    return jnp.concatenate([lo, hi], axis=1)


def _merge_kernel(c_ref, q_ref, sg_ref, x_ref, wco_ref, wpo_ref, wo_ref, g2_ref, h1_ref, xp_ref):
    D = x_ref.shape[1]
    yc = jnp.dot(c_ref[...], wco_ref[...], preferred_element_type=F32)
    yp = jnp.dot(q_ref[...], wpo_ref[...], preferred_element_type=F32)
    merged = sg_ref[:, :D].astype(F32) * yc + sg_ref[:, D:].astype(F32) * yp
    h1 = x_ref[...] + jnp.dot(merged.astype(BF16), wo_ref[...], preferred_element_type=F32)
    h1_ref[...] = h1
    xp_ref[...] = _pack_bf16_pairs(_rms(h1, g2_ref[...]))


def _merge(c2, q2, sg, x2, wco, wpo, wo, g2, tm):
    T, D = x2.shape
    C = c2.shape[1]
    full = lambda shape: pl.BlockSpec(shape, lambda i: (0, 0), pipeline_mode=pl.Buffered(1))
    return pl.pallas_call(
        _merge_kernel,
        out_shape=(jax.ShapeDtypeStruct((T, D), F32), jax.ShapeDtypeStruct((T, D // 2), I32)),
        grid=(T // tm,),
        in_specs=[
            pl.BlockSpec((tm, C), lambda i: (i, 0)),
            pl.BlockSpec((tm, C), lambda i: (i, 0)),
            pl.BlockSpec((tm, 2 * D), lambda i: (i, 0)),
            pl.BlockSpec((tm, D), lambda i: (i, 0)),
            full((C, D)), full((C, D)), full((D, D)), full((1, D)),
        ],
        out_specs=(pl.BlockSpec((tm, D), lambda i: (i, 0)), pl.BlockSpec((tm, D // 2), lambda i: (i, 0))),
        compiler_params=_cparams("parallel"),
        name="merge_outproj",
    )(c2, q2, sg, x2, wco, wpo, wo, g2)


def _router_kernel(h1_ref, g2_ref, wr_ref, br_ref, idx_ref, w_ref, rank_ref, cnt_ref, carry_ref, *, E, ts):
    i = pl.program_id(0)

    @pl.when(i == 0)
    def _():
        carry_ref[...] = jnp.zeros_like(carry_ref)

    xn = _rms(h1_ref[...], g2_ref[...])
    x_hi = xn.astype(BF16)
    x_lo = (xn - x_hi.astype(F32)).astype(BF16)
    wr = wr_ref[...]
    w_hi = wr.astype(BF16)
    w_lo = (wr - w_hi.astype(F32)).astype(BF16)
    lg = (jnp.dot(x_hi, w_hi, preferred_element_type=F32)
          + jnp.dot(x_hi, w_lo, preferred_element_type=F32)
          + jnp.dot(x_lo, w_hi, preferred_element_type=F32))
    vals = lg.T[:E, :] + br_ref[...]
    e_iota = lax.broadcasted_iota(I32, (E, ts), 0)
    tops, idxs, hots = [], [], []
    for _ in range(TOP_K):
        m = jnp.max(vals, axis=0, keepdims=True)
        idx = jnp.min(jnp.where(vals == m, e_iota, E), axis=0, keepdims=True)
        hot = e_iota == idx
        vals = jnp.where(hot, -jnp.inf, vals)
        tops.append(m)
        idxs.append(idx)
        hots.append(hot)
    exps = [jnp.exp(t - tops[0]) for t in tops]
    den = exps[0] + exps[1] + exps[2] + exps[3]
    w_ref[...] = jnp.concatenate([e / den for e in exps], axis=0)
    idx_ref[...] = jnp.concatenate(idxs, axis=0)

    mask = jnp.zeros((E, ts), F32)
    for hot in hots:
        mask = mask + hot.astype(F32)
    before = lax.broadcasted_iota(I32, (ts, ts), 0) < lax.broadcasted_iota(I32, (ts, ts), 1)
    excl = jnp.dot(mask.astype(BF16), before.astype(BF16), preferred_element_type=F32)
    base = excl + carry_ref[:, 0:1]
    ranks = [jnp.sum(jnp.where(hot, base, 0.0), axis=0, keepdims=True) for hot in hots]
    rank_ref[...] = jnp.concatenate(ranks, axis=0).astype(I32)
    carry_ref[...] = carry_ref[...] + jnp.sum(mask, axis=1, keepdims=True)
    cnt_ref[...] = carry_ref[...]


def _router(h1, g2, wr_pad, br_col, E, ts):
    T, D = h1.shape
    kern = functools.partial(_router_kernel, E=E, ts=ts)
    tok = lambda: pl.BlockSpec((TOP_K, ts), lambda i: (0, i))
    return pl.pallas_call(
        kern,
        out_shape=(jax.ShapeDtypeStruct((TOP_K, T), I32), jax.ShapeDtypeStruct((TOP_K, T), F32),
                   jax.ShapeDtypeStruct((TOP_K, T), I32), jax.ShapeDtypeStruct((E, V7X_LANES), F32)),
        grid=(T // ts,),
        in_specs=[
            pl.BlockSpec((ts, D), lambda i: (i, 0)),
            pl.BlockSpec((1, D), lambda i: (0, 0)),
            pl.BlockSpec((D, V7X_LANES), lambda i: (0, 0)),
            pl.BlockSpec((E, 1), lambda i: (0, 0)),
        ],
        out_specs=(tok(), tok(), tok(), pl.BlockSpec((E, V7X_LANES), lambda i: (0, 0))),
        scratch_shapes=[pltpu.VMEM((E, V7X_LANES), F32)],
        compiler_params=_cparams("arbitrary"),
        name="router",
    )(h1, g2, wr_pad, br_col)


def _row_copy_scatter(src_ref, dst_ref, sem, r, pos):
    return pltpu.make_async_copy(src_ref.at[pl.ds(r, 1)], dst_ref.at[pl.ds(pos, 1)], sem)


def _dispatch_kernel(pos_ref, xp_ref, init_ref, xs_ref, sem, *, ts):
    del init_ref

    def issue(r, carry):
        for k in range(TOP_K):
            _row_copy_scatter(xp_ref, xs_ref, sem, r, pos_ref[k, r]).start()
        return carry

    lax.fori_loop(0, ts, issue, 0)

    def drain(r, carry):
        for k in range(TOP_K):
            _row_copy_scatter(xp_ref, xs_ref, sem, 0, 0).wait()
        return carry

    lax.fori_loop(0, ts, drain, 0)


def _dispatch(pos, xp, rows, ts):
    T, dp = xp.shape
    init = jnp.zeros((rows, dp), I32)
    kern = functools.partial(_dispatch_kernel, ts=ts)
    return pl.pallas_call(
        kern,
        out_shape=jax.ShapeDtypeStruct((rows, dp), I32),
        grid=(T // ts,),
        in_specs=[
            pl.BlockSpec((TOP_K, ts), lambda i: (0, i), memory_space=pltpu.SMEM),
            pl.BlockSpec((ts, dp), lambda i: (i, 0)),
            pl.BlockSpec(memory_space=pl.ANY),
        ],
        out_specs=pl.BlockSpec(memory_space=pl.ANY),
        scratch_shapes=[pltpu.SemaphoreType.DMA],
        input_output_aliases={2: 0},
        compiler_params=_cparams("arbitrary"),
        name="dispatch",
    )(pos, xp, init)


def _row_copy_gather(src_ref, dst_ref, sem, k, r, pos):
    return pltpu.make_async_copy(src_ref.at[pl.ds(pos, 1)], dst_ref.at[k, pl.ds(r, 1)], sem)


def _combine_kernel(pos_ref, w_ref, h1_ref, gf_ref, y_ref, o_ref, buf_ref, sem, *, ts):
    def issue(r, carry):
        for k in range(TOP_K):
            _row_copy_gather(y_ref, buf_ref, sem, k, r, pos_ref[k, r]).start()
        return carry

    lax.fori_loop(0, ts, issue, 0)

    def drain(r, carry):
        for k in range(TOP_K):
            _row_copy_gather(y_ref, buf_ref, sem, k, 0, 0).wait()
        return carry

    lax.fori_loop(0, ts, drain, 0)
    acc = h1_ref[...]
    for k in range(TOP_K):
        acc = acc + w_ref[:, k:k + 1] * buf_ref[k]
    o_ref[...] = _rms(acc, gf_ref[...])


def _combine(pos, w_tok, h1, gf, y, ts):
    T, D = h1.shape
    kern = functools.partial(_combine_kernel, ts=ts)
    return pl.pallas_call(
        kern,
        out_shape=jax.ShapeDtypeStruct((T, D), F32),
        grid=(T // ts,),
        in_specs=[
            pl.BlockSpec((TOP_K, ts), lambda i: (0, i), memory_space=pltpu.SMEM),
            pl.BlockSpec((ts, TOP_K), lambda i: (i, 0)),
            pl.BlockSpec((ts, D), lambda i: (i, 0)),
            pl.BlockSpec((1, D), lambda i: (0, 0)),
            pl.BlockSpec(memory_space=pl.ANY),
        ],
        out_specs=pl.BlockSpec((ts, D), lambda i: (i, 0)),
        scratch_shapes=[pltpu.VMEM((TOP_K, ts, D), F32), pltpu.SemaphoreType.DMA],
        compiler_params=_cparams("arbitrary"),
        name="combine",
    )(pos, w_tok, h1, gf, y)


def _gmm1_kernel(te_ref, tsrc_ref, tvalid_ref, tfirst_ref, xs_ref, wa_ref, wb_ref, ba_ref, bb_ref,
                 o_ref, wa_s, wb_s):
    del te_ref, tsrc_ref
    m = pl.program_id(1)

    @pl.when(tfirst_ref[m] == 1)
    def _():
        wa_s[...] = wa_ref[...].astype(BF16)
        wb_s[...] = wb_ref[...].astype(BF16)

    @pl.when(tvalid_ref[m] == 1)
    def _():
        x = _unpack_bf16_pairs(xs_ref[...])
        hg = jnp.dot(x, wa_s[...], preferred_element_type=F32) + ba_ref[...]
        hl = jnp.dot(x, wb_s[...], preferred_element_type=F32) + bb_ref[...]
        hg = jnp.minimum(hg, SWIGLU_LIMIT)
        hl = jnp.clip(hl, -SWIGLU_LIMIT, SWIGLU_LIMIT)
        o_ref[...] = (hg * jax.nn.sigmoid(SWIGLU_ALPHA * hg) * (hl + 1.0)).astype(o_ref.dtype)

    @pl.when(tvalid_ref[m] == 0)
    def _():
        o_ref[...] = jnp.zeros_like(o_ref)


def _gmm1(tiles, xs, w1, b1, tm, tn):
    rows, dp = xs.shape
    E, D, F2 = w1.shape
    F = F2 // 2
    nb = F // tn
    b1r = b1.reshape(E, 1, F2)
    grid_spec = pltpu.PrefetchScalarGridSpec(
        num_scalar_prefetch=4,
        grid=(nb, rows // tm),
        in_specs=[
            pl.BlockSpec((tm, dp), lambda n, m, te, tsrc, tv, tf: (tsrc[m], 0)),
            pl.BlockSpec((None, D, tn), lambda n, m, te, tsrc, tv, tf: (te[m], 0, n)),
            pl.BlockSpec((None, D, tn), lambda n, m, te, tsrc, tv, tf: (te[m], 0, nb + n)),
            pl.BlockSpec((None, 1, tn), lambda n, m, te, tsrc, tv, tf: (te[m], 0, n)),
            pl.BlockSpec((None, 1, tn), lambda n, m, te, tsrc, tv, tf: (te[m], 0, nb + n)),
        ],
        out_specs=pl.BlockSpec((tm, tn), lambda n, m, te, tsrc, tv, tf: (m, n)),
        scratch_shapes=[pltpu.VMEM((D, tn), BF16), pltpu.VMEM((D, tn), BF16)],
    )
    return pl.pallas_call(
        _gmm1_kernel,
        out_shape=jax.ShapeDtypeStruct((rows, F), BF16),
        grid_spec=grid_spec,
        compiler_params=_cparams("arbitrary", "arbitrary"),
        name="gmm1_swiglu",
    )(*tiles, xs, w1, w1, b1r, b1r)


def _gmm2_kernel(te_ref, tsrc_ref, tvalid_ref, tfirst_ref, a_ref, w_ref, b_ref, o_ref, w_s):
    del te_ref, tsrc_ref
    m = pl.program_id(1)

    @pl.when(tfirst_ref[m] == 1)
    def _():
        w_s[...] = w_ref[...].astype(BF16)

    @pl.when(tvalid_ref[m] == 1)
    def _():
        o_ref[...] = jnp.dot(a_ref[...], w_s[...], preferred_element_type=F32) + b_ref[...]

    @pl.when(tvalid_ref[m] == 0)
    def _():
        o_ref[...] = jnp.zeros_like(o_ref)


def _gmm2(tiles, act, w2, b2, tm, tn):
    rows, F = act.shape
    E, _, D = w2.shape
    b2r = b2.reshape(E, 1, D)
    grid_spec = pltpu.PrefetchScalarGridSpec(
        num_scalar_prefetch=4,
        grid=(D // tn, rows // tm),
        in_specs=[
            pl.BlockSpec((tm, F), lambda n, m, te, tsrc, tv, tf: (tsrc[m], 0)),
            pl.BlockSpec((None, F, tn), lambda n, m, te, tsrc, tv, tf: (te[m], 0, n)),
            pl.BlockSpec((None, 1, tn), lambda n, m, te, tsrc, tv, tf: (te[m], 0, n)),
        ],
        out_specs=pl.BlockSpec((tm, tn), lambda n, m, te, tsrc, tv, tf: (m, n)),
        scratch_shapes=[pltpu.VMEM((F, tn), BF16)],
    )
    return pl.pallas_call(
        _gmm2_kernel,
        out_shape=jax.ShapeDtypeStruct((rows, D), F32),
        grid_spec=grid_spec,
        compiler_params=_cparams("arbitrary", "arbitrary"),
        name="gmm2",
    )(*tiles, act, w2, b2r)


def _tile_plan(counts, n_tiles, tm):
    E = counts.shape[0]
    ptiles = (counts + tm - 1) // tm
    ends = jnp.cumsum(ptiles)
    starts = ends - ptiles
    used = ends[-1]
    t = jnp.arange(n_tiles, dtype=I32)
    tsrc = jnp.minimum(t, jnp.maximum(used - 1, 0))
    te = jnp.minimum(jnp.searchsorted(ends, tsrc, side="right"), E - 1).astype(I32)
    tvalid = (t < used).astype(I32)
    tfirst = jnp.concatenate([jnp.ones((1,), I32), (te[1:] != te[:-1]).astype(I32)])
    return (te, tsrc.astype(I32), tvalid, tfirst), (starts * tm).astype(I32)


def kernel(x, norm_mix_g, w_in, conv_w, conv_b, conv_ln_g, conv_ln_b, w_conv_out, pool_w, pool_scale,
           w_pool_out, w_o, norm_ffn_g, w_router, b_router, w1, b1, w2, b2, norm_final_g):
    B, S, D = x.shape
    T = B * S
    assert w_in.shape[0] == 1, "the combine kernel fuses the final norm: one block only"
    E = w_router.shape[2]
    F = w2.shape[2]
    C = w_conv_out.shape[1]
    h = x.reshape(T, D)
    w_in_bf = w_in[0].astype(BF16)
    g1 = norm_mix_g[0].reshape(1, D)
    tm_in = _tile(T, 1024)
    tn_in = _tile(C, 512)
    u = _inproj(h, g1, w_in_bf, 0, C, "glu", BF16, tm_in, tn_in)
    p = _inproj(h, g1, w_in_bf, 2 * C, C, "lin", F32, tm_in, tn_in)
    sg = _inproj(h, g1, w_in_bf, 3 * C, 2 * D, "sig", BF16, tm_in, tn_in)
    ts_seq = _tile(S, 512)
    c = _conv_branch(u.reshape(B, S, C), conv_w[0], conv_b[0], conv_ln_g[0], conv_ln_b[0], ts_seq)
    q = _pool_branch(p.reshape(B, S, C), pool_w[0].astype(BF16), pool_scale[0], ts_seq)
    g2 = norm_ffn_g[0].reshape(1, D)
    h1, xp = _merge(c.reshape(T, C), q.reshape(T, C), sg, h, w_conv_out[0].astype(BF16),
                    w_pool_out[0].astype(BF16), w_o[0].astype(BF16), g2, _tile(T, 256))
    wr_pad = jnp.pad(w_router[0], ((0, 0), (0, V7X_LANES - E)))
    idx_t, w_t, rank_t, cnt = _router(h1, g2, wr_pad, b_router[0].reshape(E, 1), E, _tile(T, 512))
    tm = _tile(T, 256)
    n_tiles = (T * TOP_K) // tm + E
    tiles, row_start = _tile_plan(cnt[:, 0].astype(I32), n_tiles, tm)
    pos = row_start[idx_t] + rank_t
    xs = _dispatch(pos, xp, n_tiles * tm, _tile(T, 256))
    act = _gmm1(tiles, xs, w1[0], b1[0], tm, _tile(F, 512))
    y = _gmm2(tiles, act, w2[0], b2[0], tm, _tile(D, 1024))
    out = _combine(pos, w_t.T, h1, norm_final_g.reshape(1, D), y, _tile(T, 128))
    return out.reshape(B, S, D)
```

```python
import functools

import jax
import jax.numpy as jnp
from jax import lax
from jax.experimental import pallas as pl
from jax.experimental.pallas import tpu as pltpu

NORM_EPS = 1e-5
POOL_WINDOWS = (2, 4, 8, 16)
TOP_K = 4
SWIGLU_ALPHA = 1.702
SWIGLU_LIMIT = 7.0

V7X_LANES = 128
V7X_SUBLANES = 8
V7X_VMEM_BYTES = 64 * 1024 * 1024
VMEM_LIMIT_BYTES = V7X_VMEM_BYTES - 8 * 1024 * 1024

F32 = jnp.float32
BF16 = jnp.bfloat16
I32 = jnp.int32


def _cparams(*sem):
    return pltpu.CompilerParams(dimension_semantics=sem, vmem_limit_bytes=VMEM_LIMIT_BYTES)


def _tile(n, pref):
    t = min(n, pref)
    while n % t:
        t //= 2
    return t


def _rms(x, g):
    ms = jnp.mean(x * x, axis=-1, keepdims=True)
    return x * lax.rsqrt(ms + NORM_EPS) * g


def _inproj_prologue(x_ref, g_ref, xn_ref):
    @pl.when(pl.program_id(1) == 0)
    def _():
        xn_ref[...] = _rms(x_ref[...], g_ref[...]).astype(BF16)


def _inproj_glu_kernel(x_ref, g_ref, wa_ref, wb_ref, o_ref, xn_ref):
    _inproj_prologue(x_ref, g_ref, xn_ref)
    xn = xn_ref[...]
    a = jnp.dot(xn, wa_ref[...], preferred_element_type=F32)
    b = jnp.dot(xn, wb_ref[...], preferred_element_type=F32)
    o_ref[...] = (a * jax.nn.sigmoid(b)).astype(o_ref.dtype)


def _inproj_lin_kernel(x_ref, g_ref, w_ref, o_ref, xn_ref):
    _inproj_prologue(x_ref, g_ref, xn_ref)
    o_ref[...] = jnp.dot(xn_ref[...], w_ref[...], preferred_element_type=F32).astype(o_ref.dtype)


def _inproj_sig_kernel(x_ref, g_ref, w_ref, o_ref, xn_ref):
    _inproj_prologue(x_ref, g_ref, xn_ref)
    h = jnp.dot(xn_ref[...], w_ref[...], preferred_element_type=F32)
    o_ref[...] = jax.nn.sigmoid(h).astype(o_ref.dtype)


def _inproj(x2, g, w_bf, col0, ncols, mode, out_dtype, tm, tn):
    T, D = x2.shape
    nb = ncols // tn
    c0 = col0 // tn
    x_spec = pl.BlockSpec((tm, D), lambda i, j: (i, 0))
    g_spec = pl.BlockSpec((1, D), lambda i, j: (0, 0))
    w_spec = pl.BlockSpec((D, tn), lambda i, j: (0, c0 + j))
    if mode == "glu":
        kern = _inproj_glu_kernel
        in_specs = [x_spec, g_spec, w_spec, pl.BlockSpec((D, tn), lambda i, j: (0, c0 + nb + j))]
        args = (x2, g, w_bf, w_bf)
    else:
        kern = _inproj_lin_kernel if mode == "lin" else _inproj_sig_kernel
        in_specs = [x_spec, g_spec, w_spec]
        args = (x2, g, w_bf)
    return pl.pallas_call(
        kern,
        out_shape=jax.ShapeDtypeStruct((T, ncols), out_dtype),
        grid=(T // tm, nb),
        in_specs=in_specs,
        out_specs=pl.BlockSpec((tm, tn), lambda i, j: (i, j)),
        scratch_shapes=[pltpu.VMEM((tm, D), BF16)],
        compiler_params=_cparams("parallel", "arbitrary"),
        name="inproj_" + mode,
    )(*args)


CONV_HALO = 32
CONV_ROWS = 32
CONV_LANES = V7X_LANES


def _conv_kernel(cur_ref, halo_ref, w_ref, b_ref, g_ref, beta_ref, o_ref, ext_ref, *, ts, C, KW):
    s = pl.program_id(1)
    halo = halo_ref[...].astype(F32)
    ext_ref[0:CONV_HALO, :] = jnp.where(s > 0, halo, 0.0)
    ext_ref[CONV_HALO:, :] = cur_ref[...].astype(F32)
    off = CONV_HALO - (KW - 1)
    lc = min(C, CONV_LANES)
    sub = V7X_SUBLANES
    region_rows = CONV_ROWS + CONV_HALO

    def body(i, carry):
        r0 = pl.multiple_of(i * CONV_ROWS, CONV_ROWS)
        parts = []
        for c0 in range(0, C, lc):
            region = ext_ref[pl.ds(r0, region_rows), c0:c0 + lc]
            acc = jnp.zeros((CONV_ROWS, lc), F32) + b_ref[:, c0:c0 + lc]
            for r in range(sub):
                span = CONV_ROWS + (sub if r else 0)
                part = None
                for a in range(CONV_HALO // sub + 1):
                    k = sub * a + r - off
                    if 0 <= k < KW:
                        assert sub * a + span <= region_rows
                        term = region[sub * a:sub * a + span, :] * w_ref[k:k + 1, c0:c0 + lc]
                        part = term if part is None else part + term
                if part is not None:
                    acc = acc + part[r:r + CONV_ROWS, :]
            parts.append(acc)
        y = jnp.concatenate(parts, axis=1) if len(parts) > 1 else parts[0]
        mu = jnp.mean(y, axis=-1, keepdims=True)
        d = y - mu
        var = jnp.mean(d * d, axis=-1, keepdims=True)
        yn = d * lax.rsqrt(var + NORM_EPS) * g_ref[...] + beta_ref[...]
        o_ref[pl.ds(r0, CONV_ROWS), :] = (yn * jax.nn.sigmoid(yn)).astype(o_ref.dtype)
        return carry

    lax.fori_loop(0, ts // CONV_ROWS, body, 0)


def _conv_branch(u3, conv_w, conv_b, ln_g, ln_b, ts):
    B, S, C = u3.shape
    KW = conv_w.shape[0]
    assert KW - 1 <= CONV_HALO and ts % CONV_HALO == 0
    hb = ts // CONV_HALO
    kern = functools.partial(_conv_kernel, ts=ts, C=C, KW=KW)
    vec = lambda: pl.BlockSpec((1, C), lambda b, s: (0, 0))
    return pl.pallas_call(
        kern,
        out_shape=jax.ShapeDtypeStruct((B, S, C), BF16),
        grid=(B, S // ts),
        in_specs=[
            pl.BlockSpec((None, ts, C), lambda b, s: (b, s, 0)),
            pl.BlockSpec((None, CONV_HALO, C), lambda b, s: (b, jnp.maximum(s * hb - 1, 0), 0)),
            pl.BlockSpec((KW, C), lambda b, s: (0, 0)),
            vec(), vec(), vec(),
        ],
        out_specs=pl.BlockSpec((None, ts, C), lambda b, s: (b, s, 0)),
        scratch_shapes=[pltpu.VMEM((CONV_HALO + ts, C), F32)],
        compiler_params=_cparams("parallel", "arbitrary"),
        name="conv_branch",
    )(u3, u3, conv_w, conv_b.reshape(1, C), ln_g.reshape(1, C), ln_b.reshape(1, C))


POOL_HALO = 16
POOL_ROWS = 32


def _pool_kernel(cur_ref, halo_ref, pw_ref, ps_ref, o_ref, ext_ref, diff_ref, *, ts, C):
    s = pl.program_id(1)
    G = len(POOL_WINDOWS)
    cg = C // G
    ext_ref[0:POOL_HALO, :] = jnp.where(s > 0, halo_ref[...], 0.0)
    ext_ref[POOL_HALO:, :] = cur_ref[...]

    def body(i, carry):
        r0 = pl.multiple_of(i * POOL_ROWS, POOL_ROWS)
        row = lax.broadcasted_iota(I32, (POOL_ROWS, 1), 0)
        pos1 = (s * ts + r0 + row + 1).astype(F32)
        for gi, w in enumerate(POOL_WINDOWS):
            win = ext_ref[pl.ds(r0, POOL_ROWS + POOL_HALO), gi * cg:(gi + 1) * cg]
            tok = win[POOL_HALO:POOL_HALO + POOL_ROWS, :]
            tot = tok
            for j in range(1, w):
                tot = tot + win[POOL_HALO - j:POOL_HALO - j + POOL_ROWS, :]
            d = tot / jnp.minimum(pos1, float(w)) - tok
            diff_ref[pl.ds(r0, POOL_ROWS), gi * cg:(gi + 1) * cg] = d.astype(BF16)
        return carry

    lax.fori_loop(0, ts // POOL_ROWS, body, 0)
    for gi in range(G):
        y = jnp.dot(diff_ref[:, gi * cg:(gi + 1) * cg], pw_ref[gi], preferred_element_type=F32)
        o_ref[:, gi * cg:(gi + 1) * cg] = (y * ps_ref[:, gi * cg:(gi + 1) * cg]).astype(o_ref.dtype)


def _pool_branch(p3, pool_w_bf, pool_scale, ts):
    B, S, C = p3.shape
    G, cg, _ = pool_w_bf.shape
    assert G == len(POOL_WINDOWS) and max(POOL_WINDOWS) - 1 <= POOL_HALO and ts % POOL_ROWS == 0
    hb = ts // POOL_HALO
    kern = functools.partial(_pool_kernel, ts=ts, C=C)
    return pl.pallas_call(
        kern,
        out_shape=jax.ShapeDtypeStruct((B, S, C), BF16),
        grid=(B, S // ts),
        in_specs=[
            pl.BlockSpec((None, ts, C), lambda b, s: (b, s, 0)),
            pl.BlockSpec((None, POOL_HALO, C), lambda b, s: (b, jnp.maximum(s * hb - 1, 0), 0)),
            pl.BlockSpec((G, cg, cg), lambda b, s: (0, 0, 0)),
            pl.BlockSpec((1, C), lambda b, s: (0, 0)),
        ],
        out_specs=pl.BlockSpec((None, ts, C), lambda b, s: (b, s, 0)),
        scratch_shapes=[pltpu.VMEM((POOL_HALO + ts, C), F32), pltpu.VMEM((ts, C), BF16)],
        compiler_params=_cparams("parallel", "arbitrary"),
        name="pool_branch",
    )(p3, p3, pool_w_bf, pool_scale.reshape(1, C))


def _merge_kernel(c_ref, q_ref, sg_ref, x_ref, wco_ref, wpo_ref, wo_ref, h1_ref):
    D = x_ref.shape[1]
    yc = jnp.dot(c_ref[...], wco_ref[...], preferred_element_type=F32)
    yp = jnp.dot(q_ref[...], wpo_ref[...], preferred_element_type=F32)
    merged = sg_ref[:, :D].astype(F32) * yc + sg_ref[:, D:].astype(F32) * yp
    h1_ref[...] = x_ref[...] + jnp.dot(merged.astype(BF16), wo_ref[...], preferred_element_type=F32)


def _merge(c2, q2, sg, x2, wco, wpo, wo, tm):
    T, D = x2.shape
    C = c2.shape[1]
    full = lambda shape: pl.BlockSpec(shape, lambda i: (0, 0), pipeline_mode=pl.Buffered(1))
    return pl.pallas_call(
        _merge_kernel,
        out_shape=jax.ShapeDtypeStruct((T, D), F32),
        grid=(T // tm,),
        in_specs=[
            pl.BlockSpec((tm, C), lambda i: (i, 0)),
            pl.BlockSpec((tm, C), lambda i: (i, 0)),
            pl.BlockSpec((tm, 2 * D), lambda i: (i, 0)),
            pl.BlockSpec((tm, D), lambda i: (i, 0)),
            full((C, D)), full((C, D)), full((D, D)),
        ],
        out_specs=pl.BlockSpec((tm, D), lambda i: (i, 0)),
        compiler_params=_cparams("parallel"),
        name="merge_outproj",
    )(c2, q2, sg, x2, wco, wpo, wo)


def _router_kernel(h1_ref, g2_ref, wr_ref, br_ref, idx_ref, w_ref, rank_ref, cnt_ref, carry_ref, *, E, ts):
    i = pl.program_id(0)

    @pl.when(i == 0)
    def _():
        carry_ref[...] = jnp.zeros_like(carry_ref)

    xn = _rms(h1_ref[...], g2_ref[...])
    x_hi = xn.astype(BF16)
    x_lo = (xn - x_hi.astype(F32)).astype(BF16)
    wr = wr_ref[...]
    w_hi = wr.astype(BF16)
    w_lo = (wr - w_hi.astype(F32)).astype(BF16)
    lg = (jnp.dot(x_hi, w_hi, preferred_element_type=F32)
          + jnp.dot(x_hi, w_lo, preferred_element_type=F32)
          + jnp.dot(x_lo, w_hi, preferred_element_type=F32))
    vals = lg.T[:E, :] + br_ref[...]
    e_iota = lax.broadcasted_iota(I32, (E, ts), 0)
    tops, idxs, hots = [], [], []
    for _ in range(TOP_K):
        m = jnp.max(vals, axis=0, keepdims=True)
        idx = jnp.min(jnp.where(vals == m, e_iota, E), axis=0, keepdims=True)
        hot = e_iota == idx
        vals = jnp.where(hot, -jnp.inf, vals)
        tops.append(m)
        idxs.append(idx)
        hots.append(hot)
    exps = [jnp.exp(t - tops[0]) for t in tops]
    den = exps[0] + exps[1] + exps[2] + exps[3]
    w_ref[...] = jnp.concatenate([e / den for e in exps], axis=0)
    idx_ref[...] = jnp.concatenate(idxs, axis=0)

    mask = jnp.zeros((E, ts), F32)
    for hot in hots:
        mask = mask + hot.astype(F32)
    before = lax.broadcasted_iota(I32, (ts, ts), 0) < lax.broadcasted_iota(I32, (ts, ts), 1)
    excl = jnp.dot(mask.astype(BF16), before.astype(BF16), preferred_element_type=F32)
    base = excl + carry_ref[:, 0:1]
    ranks = [jnp.sum(jnp.where(hot, base, 0.0), axis=0, keepdims=True) for hot in hots]
    rank_ref[...] = jnp.concatenate(ranks, axis=0).astype(I32)
    carry_ref[...] = carry_ref[...] + jnp.sum(mask, axis=1, keepdims=True)
    cnt_ref[...] = carry_ref[...]


def _router(h1, g2, wr_pad, br_col, E, ts):
    T, D = h1.shape
    kern = functools.partial(_router_kernel, E=E, ts=ts)
    tok = lambda: pl.BlockSpec((TOP_K, ts), lambda i: (0, i))
    return pl.pallas_call(
        kern,
        out_shape=(jax.ShapeDtypeStruct((TOP_K, T), I32), jax.ShapeDtypeStruct((TOP_K, T), F32),
                   jax.ShapeDtypeStruct((TOP_K, T), I32), jax.ShapeDtypeStruct((E, V7X_LANES), F32)),
        grid=(T // ts,),
        in_specs=[
            pl.BlockSpec((ts, D), lambda i: (i, 0)),
            pl.BlockSpec((1, D), lambda i: (0, 0)),
            pl.BlockSpec((D, V7X_LANES), lambda i: (0, 0)),
            pl.BlockSpec((E, 1), lambda i: (0, 0)),
        ],
        out_specs=(tok(), tok(), tok(), pl.BlockSpec((E, V7X_LANES), lambda i: (0, 0))),
        scratch_shapes=[pltpu.VMEM((E, V7X_LANES), F32)],
        compiler_params=_cparams("arbitrary"),
        name="router",
    )(h1, g2, wr_pad, br_col)


def _pos_kernel(start_ref, idx_ref, rank_ref, pos_ref, *, E):
    idx = idx_ref[...]
    base = jnp.zeros_like(idx)
    for e in range(E):
        base = jnp.where(idx == e, start_ref[e], base)
    pos_ref[...] = base + rank_ref[...]


def _positions(row_start, idx_t, rank_t, tl):
    K, T = idx_t.shape
    E = row_start.shape[0]
    tok = lambda: pl.BlockSpec((K, tl), lambda i: (0, i))
    return pl.pallas_call(
        functools.partial(_pos_kernel, E=E),
        out_shape=jax.ShapeDtypeStruct((K, T), I32),
        grid=(T // tl,),
        in_specs=[pl.BlockSpec(memory_space=pltpu.SMEM), tok(), tok()],
        out_specs=tok(),
        compiler_params=_cparams("parallel"),
        name="positions",
    )(row_start, idx_t, rank_t)


ROW_COPY_UNROLL = 4


def _row_copy_scatter(src_ref, dst_ref, sem, r, pos):
    return pltpu.make_async_copy(src_ref.at[pl.ds(r, 1)], dst_ref.at[pl.ds(pos, 1)], sem)


def _tile_fill(zero_ref, dst_ref, sem, t, tm):
    return pltpu.make_async_copy(zero_ref, dst_ref.at[pl.ds(pl.multiple_of(t * tm, tm), tm)], sem)


def _dispatch_kernel(zflag_ref, pos_ref, h1_ref, g2_ref, xs_ref, xn_ref, zero_ref, sem, zsem,
                     *, ts, tm, n_tiles):
    xn_ref[...] = _rms(h1_ref[...], g2_ref[...])

    @pl.when(pl.program_id(0) == 0)
    def _():
        zero_ref[...] = jnp.zeros_like(zero_ref)

        def fill(t, carry):
            @pl.when(zflag_ref[t] == 1)
            def _():
                _tile_fill(zero_ref, xs_ref, zsem, t, tm).start()
            return carry

        lax.fori_loop(0, n_tiles, fill, 0)

        def fill_wait(t, carry):
            @pl.when(zflag_ref[t] == 1)
            def _():
                _tile_fill(zero_ref, xs_ref, zsem, t, tm).wait()
            return carry

        lax.fori_loop(0, n_tiles, fill_wait, 0)

    def issue(r, carry):
        for k in range(TOP_K):
            _row_copy_scatter(xn_ref, xs_ref, sem, r, pos_ref[0, k * ts + r]).start(priority=k % 2)
        return carry

    lax.fori_loop(0, ts, issue, 0, unroll=ROW_COPY_UNROLL)

    def drain(r, carry):
        for k in range(TOP_K):
            _row_copy_scatter(xn_ref, xs_ref, sem, 0, 0).wait()
        return carry

    lax.fori_loop(0, ts, drain, 0, unroll=ROW_COPY_UNROLL)


def _tile_major(pos, ts):
    K, T = pos.shape
    return pos.reshape(K, T // ts, ts).transpose(1, 0, 2).reshape(T // ts, 1, K * ts)


def _dispatch(zflag, pos, h1, g2, n_tiles, tm, ts):
    T, D = h1.shape
    kern = functools.partial(_dispatch_kernel, ts=ts, tm=tm, n_tiles=n_tiles)
    grid_spec = pltpu.PrefetchScalarGridSpec(
        num_scalar_prefetch=1,
        grid=(T // ts,),
        in_specs=[
            pl.BlockSpec((None, 1, TOP_K * ts), lambda i, zf: (i, 0, 0), memory_space=pltpu.SMEM),
            pl.BlockSpec((ts, D), lambda i, zf: (i, 0)),
            pl.BlockSpec((1, D), lambda i, zf: (0, 0)),
        ],
        out_specs=pl.BlockSpec(memory_space=pl.ANY),
        scratch_shapes=[pltpu.VMEM((ts, D), F32), pltpu.VMEM((tm, D), F32),
                        pltpu.SemaphoreType.DMA, pltpu.SemaphoreType.DMA],
    )
    return pl.pallas_call(
        kern,
        out_shape=jax.ShapeDtypeStruct((n_tiles * tm, D), F32),
        grid_spec=grid_spec,
        compiler_params=_cparams("arbitrary"),
        name="dispatch",
    )(zflag, _tile_major(pos, ts), h1, g2)


def _row_copy_gather(src_ref, dst_ref, sem, slot, k, r, pos):
    return pltpu.make_async_copy(src_ref.at[pl.ds(pos, 1)], dst_ref.at[slot, k, pl.ds(r, 1)], sem.at[slot])


def _combine_kernel(pos_ref, nxt_ref, w_ref, h1_ref, gf_ref, y_ref, o_ref, buf_ref, sem, *, ts):
    i = pl.program_id(0)
    slot = i & 1

    def gather_tile(table_ref, dst_slot):
        def issue(r, carry):
            for k in range(TOP_K):
                _row_copy_gather(y_ref, buf_ref, sem, dst_slot, k, r, table_ref[0, k * ts + r]).start(
                    priority=k % 2)
            return carry

        lax.fori_loop(0, ts, issue, 0, unroll=ROW_COPY_UNROLL)

    @pl.when(i == 0)
    def _():
        gather_tile(pos_ref, 0)

    @pl.when(i + 1 < pl.num_programs(0))
    def _():
        gather_tile(nxt_ref, 1 - slot)

    def drain(r, carry):
        for k in range(TOP_K):
            _row_copy_gather(y_ref, buf_ref, sem, slot, k, 0, 0).wait()
        return carry

    lax.fori_loop(0, ts, drain, 0, unroll=ROW_COPY_UNROLL)
    acc = h1_ref[...]
    for k in range(TOP_K):
        acc = acc + w_ref[:, k:k + 1] * buf_ref[slot, k]
    o_ref[...] = _rms(acc, gf_ref[...])


def _combine(pos, w_tok, h1, gf, y, ts):
    T, D = h1.shape
    n = T // ts
    kern = functools.partial(_combine_kernel, ts=ts)
    pos_tiles = _tile_major(pos, ts)
    return pl.pallas_call(
        kern,
        out_shape=jax.ShapeDtypeStruct((T, D), F32),
        grid=(n,),
        in_specs=[
            pl.BlockSpec((None, 1, TOP_K * ts), lambda i: (i, 0, 0), memory_space=pltpu.SMEM),
            pl.BlockSpec((None, 1, TOP_K * ts), lambda i: (jnp.minimum(i + 1, n - 1), 0, 0),
                         memory_space=pltpu.SMEM),
            pl.BlockSpec((ts, TOP_K), lambda i: (i, 0)),
            pl.BlockSpec((ts, D), lambda i: (i, 0)),
            pl.BlockSpec((1, D), lambda i: (0, 0)),
            pl.BlockSpec(memory_space=pl.ANY),
        ],
        out_specs=pl.BlockSpec((ts, D), lambda i: (i, 0)),
        scratch_shapes=[pltpu.VMEM((2, TOP_K, ts, D), F32), pltpu.SemaphoreType.DMA((2,))],
        compiler_params=_cparams("arbitrary"),
        name="combine",
    )(pos_tiles, pos_tiles, w_tok, h1, gf, y)


TILE_READ_PRIORITY = 1


def _tile_read(in_hbm, in_buf, in_sem, slot, t, tm):
    return pltpu.make_async_copy(in_hbm.at[pl.ds(pl.multiple_of(t * tm, tm), tm)], in_buf.at[slot],
                                 in_sem.at[slot])


def _group_tile_loop(t0, nt, used, n_tiles, last_group, in_hbm, out_hbm, in_buf, out_buf, in_sem, out_sem,
                     col0, tm, tn, compute):
    def in_copy(slot, t):
        return _tile_read(in_hbm, in_buf, in_sem, slot, t, tm)

    def out_copy(slot, t):
        dst = out_hbm.at[pl.ds(pl.multiple_of(t * tm, tm), tm), pl.ds(col0, tn)]
        return pltpu.make_async_copy(out_buf.at[slot], dst, out_sem.at[slot])

    def body(j, carry):
        slot = j & 1
        in_copy(slot, t0 + j).wait()

        @pl.when(j + 1 < nt)
        def _():
            in_copy(1 - slot, t0 + j + 1).start(priority=TILE_READ_PRIORITY)

        res = compute(in_buf[slot])

        @pl.when(j >= 2)
        def _():
            out_copy(slot, t0 + j - 2).wait()

        out_buf[slot] = res
        out_copy(slot, t0 + j).start()
        return carry

    lax.fori_loop(0, nt, body, 0)

    @pl.when(nt >= 2)
    def _():
        out_copy(nt & 1, t0 + nt - 2).wait()

    @pl.when(nt >= 1)
    def _():
        out_copy((nt - 1) & 1, t0 + nt - 1).wait()

    @pl.when(last_group)
    def _():
        out_buf[0] = jnp.zeros(out_buf.shape[1:], out_buf.dtype)

        def fill(t, carry):
            out_copy(0, t).start()
            out_copy(0, t).wait()
            return carry

        lax.fori_loop(used, n_tiles, fill, 0)


def _gmm1_kernel(t0_ref, nt_ref, wa_ref, wb_ref, ba_ref, bb_ref, xs_ref, o_ref,
                 wa_s, wb_s, in_buf, out_buf, in_sem, out_sem, *, tm, tn, n_tiles):
    n = pl.program_id(0)
    e = pl.program_id(1)
    E = pl.num_programs(1)
    t0 = t0_ref[e]
    nt = nt_ref[e]

    @pl.when(nt > 0)
    def _():
        _tile_read(xs_ref, in_buf, in_sem, 0, t0, tm).start(priority=TILE_READ_PRIORITY)
        wa_s[...] = wa_ref[...].astype(BF16)
        wb_s[...] = wb_ref[...].astype(BF16)

    def compute(xt):
        x = xt.astype(BF16)
        hg = jnp.dot(x, wa_s[...], preferred_element_type=F32) + ba_ref[...]
        hl = jnp.dot(x, wb_s[...], preferred_element_type=F32) + bb_ref[...]
        hg = jnp.minimum(hg, SWIGLU_LIMIT)
        hl = jnp.clip(hl, -SWIGLU_LIMIT, SWIGLU_LIMIT)
        return (hg * jax.nn.sigmoid(SWIGLU_ALPHA * hg) * (hl + 1.0)).astype(BF16)

    used = t0_ref[E - 1] + nt_ref[E - 1]
    _group_tile_loop(t0, nt, used, n_tiles, e == E - 1, xs_ref, o_ref, in_buf, out_buf, in_sem, out_sem,
                     pl.multiple_of(n * tn, tn), tm, tn, compute)


def _gmm1(t0, nt, xs, w1, b1, tm, tn):
    rows = xs.shape[0]
    E, D, F2 = w1.shape
    F = F2 // 2
    nb = F // tn
    b1r = b1.reshape(E, 1, F2)
    kern = functools.partial(_gmm1_kernel, tm=tm, tn=tn, n_tiles=rows // tm)
    grid_spec = pltpu.PrefetchScalarGridSpec(
        num_scalar_prefetch=2,
        grid=(nb, E),
        in_specs=[
            pl.BlockSpec((None, D, tn), lambda n, e, t0, nt: (e, 0, n)),
            pl.BlockSpec((None, D, tn), lambda n, e, t0, nt: (e, 0, nb + n)),
            pl.BlockSpec((None, 1, tn), lambda n, e, t0, nt: (e, 0, n)),
            pl.BlockSpec((None, 1, tn), lambda n, e, t0, nt: (e, 0, nb + n)),
            pl.BlockSpec(memory_space=pl.ANY),
        ],
        out_specs=pl.BlockSpec(memory_space=pl.ANY),
        scratch_shapes=[
            pltpu.VMEM((D, tn), BF16), pltpu.VMEM((D, tn), BF16),
            pltpu.VMEM((2, tm, D), F32), pltpu.VMEM((2, tm, tn), BF16),
            pltpu.SemaphoreType.DMA((2,)), pltpu.SemaphoreType.DMA((2,)),
        ],
    )
    return pl.pallas_call(
        kern,
        out_shape=jax.ShapeDtypeStruct((rows, F), BF16),
        grid_spec=grid_spec,
        compiler_params=_cparams("arbitrary", "arbitrary"),
        name="gmm1_swiglu",
    )(t0, nt, w1, w1, b1r, b1r, xs)


def _gmm2_kernel(t0_ref, nt_ref, w_ref, b_ref, a_ref, o_ref, w_s, in_buf, out_buf, in_sem, out_sem,
                 *, tm, tn, n_tiles):
    n = pl.program_id(0)
    e = pl.program_id(1)
    E = pl.num_programs(1)
    t0 = t0_ref[e]
    nt = nt_ref[e]

    @pl.when(nt > 0)
    def _():
        _tile_read(a_ref, in_buf, in_sem, 0, t0, tm).start(priority=TILE_READ_PRIORITY)
        w_s[...] = w_ref[...].astype(BF16)

    def compute(a):
        return jnp.dot(a, w_s[...], preferred_element_type=F32) + b_ref[...]

    used = t0_ref[E - 1] + nt_ref[E - 1]
    _group_tile_loop(t0, nt, used, n_tiles, e == E - 1, a_ref, o_ref, in_buf, out_buf, in_sem, out_sem,
                     pl.multiple_of(n * tn, tn), tm, tn, compute)


def _gmm2(t0, nt, act, w2, b2, tm, tn):
    rows, F = act.shape
    E, _, D = w2.shape
    b2r = b2.reshape(E, 1, D)
    kern = functools.partial(_gmm2_kernel, tm=tm, tn=tn, n_tiles=rows // tm)
    grid_spec = pltpu.PrefetchScalarGridSpec(
        num_scalar_prefetch=2,
        grid=(D // tn, E),
        in_specs=[
            pl.BlockSpec((None, F, tn), lambda n, e, t0, nt: (e, 0, n)),
            pl.BlockSpec((None, 1, tn), lambda n, e, t0, nt: (e, 0, n)),
            pl.BlockSpec(memory_space=pl.ANY),
        ],
        out_specs=pl.BlockSpec(memory_space=pl.ANY),
        scratch_shapes=[
            pltpu.VMEM((F, tn), BF16),
            pltpu.VMEM((2, tm, F), BF16), pltpu.VMEM((2, tm, tn), F32),
            pltpu.SemaphoreType.DMA((2,)), pltpu.SemaphoreType.DMA((2,)),
        ],
    )
    return pl.pallas_call(
        kern,
        out_shape=jax.ShapeDtypeStruct((rows, D), F32),
        grid_spec=grid_spec,
        compiler_params=_cparams("arbitrary", "arbitrary"),
        name="gmm2",
    )(t0, nt, w2, b2r, act)


def _tile_plan(counts, n_tiles, tm):
    ptiles = (counts + tm - 1) // tm
    ends = jnp.cumsum(ptiles)
    starts = ends - ptiles
    t = jnp.arange(n_tiles, dtype=I32)[:, None]
    last_of_group = jnp.any((ptiles[None, :] > 0) & (t == ends[None, :] - 1), axis=1)
    zflag = (last_of_group | (t[:, 0] >= ends[-1])).astype(I32)
    return starts.astype(I32), ptiles.astype(I32), (starts * tm).astype(I32), zflag


def kernel(x, norm_mix_g, w_in, conv_w, conv_b, conv_ln_g, conv_ln_b, w_conv_out, pool_w, pool_scale,
           w_pool_out, w_o, norm_ffn_g, w_router, b_router, w1, b1, w2, b2, norm_final_g):
    B, S, D = x.shape
    T = B * S
    assert w_in.shape[0] == 1, "the combine kernel fuses the final norm: one block only"
    E = w_router.shape[2]
    F = w2.shape[2]
    C = w_conv_out.shape[1]
    h = x.reshape(T, D)
    w_in_bf = w_in[0].astype(BF16)
    g1 = norm_mix_g[0].reshape(1, D)
    tm_in = _tile(T, 1024)
    tn_in = _tile(C, 512)
    u = _inproj(h, g1, w_in_bf, 0, C, "glu", BF16, tm_in, tn_in)
    p = _inproj(h, g1, w_in_bf, 2 * C, C, "lin", F32, tm_in, tn_in)
    sg = _inproj(h, g1, w_in_bf, 3 * C, 2 * D, "sig", BF16, tm_in, tn_in)
    ts_seq = _tile(S, 512)
    c = _conv_branch(u.reshape(B, S, C), conv_w[0], conv_b[0], conv_ln_g[0], conv_ln_b[0], ts_seq)
    q = _pool_branch(p.reshape(B, S, C), pool_w[0].astype(BF16), pool_scale[0], ts_seq)
    g2 = norm_ffn_g[0].reshape(1, D)
    h1 = _merge(c.reshape(T, C), q.reshape(T, C), sg, h, w_conv_out[0].astype(BF16),
                w_pool_out[0].astype(BF16), w_o[0].astype(BF16), _tile(T, 256))
    wr_pad = jnp.pad(w_router[0], ((0, 0), (0, V7X_LANES - E)))
    idx_t, w_t, rank_t, cnt = _router(h1, g2, wr_pad, b_router[0].reshape(E, 1), E, _tile(T, 512))
    tm = _tile(T, 256)
    n_tiles = (T * TOP_K) // tm + E
    t0, nt, row_start, zflag = _tile_plan(cnt[:, 0].astype(I32), n_tiles, tm)
    pos = _positions(row_start, idx_t, rank_t, _tile(T, 2048))
    xs = _dispatch(zflag, pos, h1, g2, n_tiles, tm, _tile(T, 512))
    act = _gmm1(t0, nt, xs, w1[0], b1[0], tm, _tile(F, 1024))
    y = _gmm2(t0, nt, act, w2[0], b2[0], tm, _tile(D, 2048))
    out = _combine(pos, w_t.T, h1, norm_final_g.reshape(1, D), y, _tile(T, 256))
    return out.reshape(B, S, D)
```

```python
import functools

import jax
import jax.numpy as jnp
from jax import lax
from jax.experimental import pallas as pl
from jax.experimental.pallas import tpu as pltpu

NORM_EPS = 1e-5
POOL_WINDOWS = (2, 4, 8, 16)
TOP_K = 4
SWIGLU_ALPHA = 1.702
SWIGLU_LIMIT = 7.0

V7X_LANES = 128
V7X_SUBLANES = 8
V7X_VMEM_BYTES = 64 * 1024 * 1024
VMEM_LIMIT_BYTES = V7X_VMEM_BYTES - 8 * 1024 * 1024

F32 = jnp.float32
BF16 = jnp.bfloat16
I32 = jnp.int32


def _cparams(*sem):
    return pltpu.CompilerParams(dimension_semantics=sem, vmem_limit_bytes=VMEM_LIMIT_BYTES)


def _tile(n, pref):
    t = min(n, pref)
    while n % t:
        t //= 2
    return t


def _rms(x, g):
    ms = jnp.mean(x * x, axis=-1, keepdims=True)
    return x * lax.rsqrt(ms + NORM_EPS) * g


def _inproj_kernel(x_ref, g_ref, wa_ref, wb_ref, u_ref, p_ref, sg_ref, xn_ref, *, nu, npool):
    j = pl.program_id(1)

    @pl.when(j == 0)
    def _():
        xn_ref[...] = _rms(x_ref[...], g_ref[...]).astype(BF16)

    xn = xn_ref[...]
    a = jnp.dot(xn, wa_ref[...], preferred_element_type=F32)

    @pl.when(j < nu)
    def _():
        b = jnp.dot(xn, wb_ref[...], preferred_element_type=F32)
        u_ref[...] = (a * jax.nn.sigmoid(b)).astype(u_ref.dtype)

    @pl.when((j >= nu) & (j < nu + npool))
    def _():
        p_ref[...] = a

    @pl.when(j >= nu + npool)
    def _():
        sg_ref[...] = jax.nn.sigmoid(a).astype(sg_ref.dtype)


def _inproj(x2, g, w_bf, C, tm, tn):
    T, D = x2.shape
    nu = C // tn
    npool = C // tn
    nsg = (w_bf.shape[1] - 3 * C) // tn
    kern = functools.partial(_inproj_kernel, nu=nu, npool=npool)
    return pl.pallas_call(
        kern,
        out_shape=(jax.ShapeDtypeStruct((T, C), BF16), jax.ShapeDtypeStruct((T, C), F32),
                   jax.ShapeDtypeStruct((T, nsg * tn), BF16)),
        grid=(T // tm, nu + npool + nsg),
        in_specs=[
            pl.BlockSpec((tm, D), lambda i, j: (i, 0)),
            pl.BlockSpec((1, D), lambda i, j: (0, 0)),
            pl.BlockSpec((D, tn), lambda i, j: (0, jnp.where(j < nu, j, j + nu))),
            pl.BlockSpec((D, tn), lambda i, j: (0, nu + jnp.minimum(j, nu - 1))),
        ],
        out_specs=(
            pl.BlockSpec((tm, tn), lambda i, j: (i, jnp.minimum(j, nu - 1))),
            pl.BlockSpec((tm, tn), lambda i, j: (i, jnp.clip(j - nu, 0, npool - 1))),
            pl.BlockSpec((tm, tn), lambda i, j: (i, jnp.clip(j - nu - npool, 0, nsg - 1))),
        ),
        scratch_shapes=[pltpu.VMEM((tm, D), BF16)],
        compiler_params=_cparams("parallel", "arbitrary"),
        name="inproj",
    )(x2, g, w_bf, w_bf)


CONV_HALO = 32
CONV_ROWS = 32
CONV_LANES = V7X_LANES


def _conv_kernel(cur_ref, halo_ref, w_ref, b_ref, g_ref, beta_ref, o_ref, ext_ref, *, ts, C, KW):
    s = pl.program_id(1)
    halo = halo_ref[...].astype(F32)
    ext_ref[0:CONV_HALO, :] = jnp.where(s > 0, halo, 0.0)
    ext_ref[CONV_HALO:, :] = cur_ref[...].astype(F32)
    off = CONV_HALO - (KW - 1)
    lc = min(C, CONV_LANES)
    sub = V7X_SUBLANES
    region_rows = CONV_ROWS + CONV_HALO

    def body(i, carry):
        r0 = pl.multiple_of(i * CONV_ROWS, CONV_ROWS)
        parts = []
        for c0 in range(0, C, lc):
            region = ext_ref[pl.ds(r0, region_rows), c0:c0 + lc]
            acc = jnp.zeros((CONV_ROWS, lc), F32) + b_ref[:, c0:c0 + lc]
            for r in range(sub):
                span = CONV_ROWS + (sub if r else 0)
                part = None
                for a in range(CONV_HALO // sub + 1):
                    k = sub * a + r - off
                    if 0 <= k < KW:
                        assert sub * a + span <= region_rows
                        term = region[sub * a:sub * a + span, :] * w_ref[k:k + 1, c0:c0 + lc]
                        part = term if part is None else part + term
                if part is not None:
                    acc = acc + part[r:r + CONV_ROWS, :]
            parts.append(acc)
        y = jnp.concatenate(parts, axis=1) if len(parts) > 1 else parts[0]
        mu = jnp.mean(y, axis=-1, keepdims=True)
        d = y - mu
        var = jnp.mean(d * d, axis=-1, keepdims=True)
        yn = d * lax.rsqrt(var + NORM_EPS) * g_ref[...] + beta_ref[...]
        o_ref[pl.ds(r0, CONV_ROWS), :] = (yn * jax.nn.sigmoid(yn)).astype(o_ref.dtype)
        return carry

    lax.fori_loop(0, ts // CONV_ROWS, body, 0)


def _conv_branch(u3, conv_w, conv_b, ln_g, ln_b, ts):
    B, S, C = u3.shape
    KW = conv_w.shape[0]
    assert KW - 1 <= CONV_HALO and ts % CONV_HALO == 0
    hb = ts // CONV_HALO
    kern = functools.partial(_conv_kernel, ts=ts, C=C, KW=KW)
    vec = lambda: pl.BlockSpec((1, C), lambda b, s: (0, 0))
    return pl.pallas_call(
        kern,
        out_shape=jax.ShapeDtypeStruct((B, S, C), BF16),
        grid=(B, S // ts),
        in_specs=[
            pl.BlockSpec((None, ts, C), lambda b, s: (b, s, 0)),
            pl.BlockSpec((None, CONV_HALO, C), lambda b, s: (b, jnp.maximum(s * hb - 1, 0), 0)),
            pl.BlockSpec((KW, C), lambda b, s: (0, 0)),
            vec(), vec(), vec(),
        ],
        out_specs=pl.BlockSpec((None, ts, C), lambda b, s: (b, s, 0)),
        scratch_shapes=[pltpu.VMEM((CONV_HALO + ts, C), F32)],
        compiler_params=_cparams("parallel", "arbitrary"),
        name="conv_branch",
    )(u3, u3, conv_w, conv_b.reshape(1, C), ln_g.reshape(1, C), ln_b.reshape(1, C))


POOL_HALO = 16
POOL_ROWS = 32


def _pool_kernel(cur_ref, halo_ref, pw_ref, ps_ref, o_ref, ext_ref, diff_ref, *, ts, C):
    s = pl.program_id(1)
    G = len(POOL_WINDOWS)
    cg = C // G
    ext_ref[0:POOL_HALO, :] = jnp.where(s > 0, halo_ref[...], 0.0)
    ext_ref[POOL_HALO:, :] = cur_ref[...]

    def body(i, carry):
        r0 = pl.multiple_of(i * POOL_ROWS, POOL_ROWS)
        row = lax.broadcasted_iota(I32, (POOL_ROWS, 1), 0)
        pos1 = (s * ts + r0 + row + 1).astype(F32)
        for gi, w in enumerate(POOL_WINDOWS):
            win = ext_ref[pl.ds(r0, POOL_ROWS + POOL_HALO), gi * cg:(gi + 1) * cg]
            tok = win[POOL_HALO:POOL_HALO + POOL_ROWS, :]
            tot = tok
            for j in range(1, w):
                tot = tot + win[POOL_HALO - j:POOL_HALO - j + POOL_ROWS, :]
            d = tot / jnp.minimum(pos1, float(w)) - tok
            diff_ref[pl.ds(r0, POOL_ROWS), gi * cg:(gi + 1) * cg] = d.astype(BF16)
        return carry

    lax.fori_loop(0, ts // POOL_ROWS, body, 0)
    for gi in range(G):
        y = jnp.dot(diff_ref[:, gi * cg:(gi + 1) * cg], pw_ref[gi], preferred_element_type=F32)
        o_ref[:, gi * cg:(gi + 1) * cg] = (y * ps_ref[:, gi * cg:(gi + 1) * cg]).astype(o_ref.dtype)


def _pool_branch(p3, pool_w_bf, pool_scale, ts):
    B, S, C = p3.shape
    G, cg, _ = pool_w_bf.shape
    assert G == len(POOL_WINDOWS) and max(POOL_WINDOWS) - 1 <= POOL_HALO and ts % POOL_ROWS == 0
    hb = ts // POOL_HALO
    kern = functools.partial(_pool_kernel, ts=ts, C=C)
    return pl.pallas_call(
        kern,
        out_shape=jax.ShapeDtypeStruct((B, S, C), BF16),
        grid=(B, S // ts),
        in_specs=[
            pl.BlockSpec((None, ts, C), lambda b, s: (b, s, 0)),
            pl.BlockSpec((None, POOL_HALO, C), lambda b, s: (b, jnp.maximum(s * hb - 1, 0), 0)),
            pl.BlockSpec((G, cg, cg), lambda b, s: (0, 0, 0)),
            pl.BlockSpec((1, C), lambda b, s: (0, 0)),
        ],
        out_specs=pl.BlockSpec((None, ts, C), lambda b, s: (b, s, 0)),
        scratch_shapes=[pltpu.VMEM((POOL_HALO + ts, C), F32), pltpu.VMEM((ts, C), BF16)],
        compiler_params=_cparams("parallel", "arbitrary"),
        name="pool_branch",
    )(p3, p3, pool_w_bf, pool_scale.reshape(1, C))


def _merge_kernel(c_ref, q_ref, sg_ref, x_ref, wco_ref, wpo_ref, wo_ref, h1_ref):
    D = x_ref.shape[1]
    yc = jnp.dot(c_ref[...], wco_ref[...], preferred_element_type=F32)
    yp = jnp.dot(q_ref[...], wpo_ref[...], preferred_element_type=F32)
    merged = sg_ref[:, :D].astype(F32) * yc + sg_ref[:, D:].astype(F32) * yp
    h1_ref[...] = x_ref[...] + jnp.dot(merged.astype(BF16), wo_ref[...], preferred_element_type=F32)


def _merge(c2, q2, sg, x2, wco, wpo, wo, tm):
    T, D = x2.shape
    C = c2.shape[1]
    full = lambda shape: pl.BlockSpec(shape, lambda i: (0, 0), pipeline_mode=pl.Buffered(1))
    return pl.pallas_call(
        _merge_kernel,
        out_shape=jax.ShapeDtypeStruct((T, D), F32),
        grid=(T // tm,),
        in_specs=[
            pl.BlockSpec((tm, C), lambda i: (i, 0)),
            pl.BlockSpec((tm, C), lambda i: (i, 0)),
            pl.BlockSpec((tm, 2 * D), lambda i: (i, 0)),
            pl.BlockSpec((tm, D), lambda i: (i, 0)),
            full((C, D)), full((C, D)), full((D, D)),
        ],
        out_specs=pl.BlockSpec((tm, D), lambda i: (i, 0)),
        compiler_params=_cparams("parallel"),
        name="merge_outproj",
    )(c2, q2, sg, x2, wco, wpo, wo)


def _router_kernel(h1_ref, g2_ref, wr_ref, br_ref, idx_ref, w_ref, rank_ref, cnt_ref, carry_ref, *, E, ts):
    i = pl.program_id(0)

    @pl.when(i == 0)
    def _():
        carry_ref[...] = jnp.zeros_like(carry_ref)

    xn = _rms(h1_ref[...], g2_ref[...])
    x_hi = xn.astype(BF16)
    x_lo = (xn - x_hi.astype(F32)).astype(BF16)
    wr = wr_ref[...]
    w_hi = wr.astype(BF16)
    w_lo = (wr - w_hi.astype(F32)).astype(BF16)
    lg = (jnp.dot(x_hi, w_hi, preferred_element_type=F32)
          + jnp.dot(x_hi, w_lo, preferred_element_type=F32)
          + jnp.dot(x_lo, w_hi, preferred_element_type=F32))
    vals = lg.T[:E, :] + br_ref[...]
    e_iota = lax.broadcasted_iota(I32, (E, ts), 0)
    tops, idxs, hots = [], [], []
    for _ in range(TOP_K):
        m = jnp.max(vals, axis=0, keepdims=True)
        idx = jnp.min(jnp.where(vals == m, e_iota, E), axis=0, keepdims=True)
        hot = e_iota == idx
        vals = jnp.where(hot, -jnp.inf, vals)
        tops.append(m)
        idxs.append(idx)
        hots.append(hot)
    exps = [jnp.exp(t - tops[0]) for t in tops]
    den = exps[0] + exps[1] + exps[2] + exps[3]
    w_ref[...] = jnp.concatenate([e / den for e in exps], axis=0)
    idx_ref[...] = jnp.concatenate(idxs, axis=0)

    mask = jnp.zeros((E, ts), F32)
    for hot in hots:
        mask = mask + hot.astype(F32)
    before = lax.broadcasted_iota(I32, (ts, ts), 0) < lax.broadcasted_iota(I32, (ts, ts), 1)
    excl = jnp.dot(mask.astype(BF16), before.astype(BF16), preferred_element_type=F32)
    base = excl + carry_ref[:, 0:1]
    ranks = [jnp.sum(jnp.where(hot, base, 0.0), axis=0, keepdims=True) for hot in hots]
    rank_ref[...] = jnp.concatenate(ranks, axis=0).astype(I32)
    carry_ref[...] = carry_ref[...] + jnp.sum(mask, axis=1, keepdims=True)
    cnt_ref[...] = carry_ref[...]


def _router(h1, g2, wr_pad, br_col, E, ts):
    T, D = h1.shape
    kern = functools.partial(_router_kernel, E=E, ts=ts)
    tok = lambda: pl.BlockSpec((TOP_K, ts), lambda i: (0, i))
    return pl.pallas_call(
        kern,
        out_shape=(jax.ShapeDtypeStruct((TOP_K, T), I32), jax.ShapeDtypeStruct((TOP_K, T), F32),
                   jax.ShapeDtypeStruct((TOP_K, T), I32), jax.ShapeDtypeStruct((E, V7X_LANES), F32)),
        grid=(T // ts,),
        in_specs=[
            pl.BlockSpec((ts, D), lambda i: (i, 0)),
            pl.BlockSpec((1, D), lambda i: (0, 0)),
            pl.BlockSpec((D, V7X_LANES), lambda i: (0, 0)),
            pl.BlockSpec((E, 1), lambda i: (0, 0)),
        ],
        out_specs=(tok(), tok(), tok(), pl.BlockSpec((E, V7X_LANES), lambda i: (0, 0))),
        scratch_shapes=[pltpu.VMEM((E, V7X_LANES), F32)],
        compiler_params=_cparams("arbitrary"),
        name="router",
    )(h1, g2, wr_pad, br_col)


def _pos_kernel(start_ref, idx_ref, rank_ref, pos_ref, *, E):
    idx = idx_ref[...]
    base = jnp.zeros_like(idx)
    for e in range(E):
        base = jnp.where(idx == e, start_ref[e], base)
    pos_ref[...] = base + rank_ref[...]


def _positions(row_start, idx_t, rank_t, tl):
    K, T = idx_t.shape
    E = row_start.shape[0]
    tok = lambda: pl.BlockSpec((K, tl), lambda i: (0, i))
    return pl.pallas_call(
        functools.partial(_pos_kernel, E=E),
        out_shape=jax.ShapeDtypeStruct((K, T), I32),
        grid=(T // tl,),
        in_specs=[pl.BlockSpec(memory_space=pltpu.SMEM), tok(), tok()],
        out_specs=tok(),
        compiler_params=_cparams("parallel"),
        name="positions",
    )(row_start, idx_t, rank_t)


ROW_COPY_UNROLL = 4


def _row_copy_scatter(src_ref, dst_ref, sem, r, pos):
    return pltpu.make_async_copy(src_ref.at[pl.ds(r, 1)], dst_ref.at[pl.ds(pos, 1)], sem)


def _tile_fill(zero_ref, dst_ref, sem, t, tm):
    return pltpu.make_async_copy(zero_ref, dst_ref.at[pl.ds(pl.multiple_of(t * tm, tm), tm)], sem)


def _dispatch_kernel(zflag_ref, pos_ref, h1_ref, g2_ref, xs_ref, xn_ref, zero_ref, sem, zsem,
                     *, ts, tm, n_tiles):
    xn_ref[...] = _rms(h1_ref[...], g2_ref[...])

    @pl.when(pl.program_id(0) == 0)
    def _():
        zero_ref[...] = jnp.zeros_like(zero_ref)

        def fill(t, carry):
            @pl.when(zflag_ref[t] == 1)
            def _():
                _tile_fill(zero_ref, xs_ref, zsem, t, tm).start()
            return carry

        lax.fori_loop(0, n_tiles, fill, 0)

        def fill_wait(t, carry):
            @pl.when(zflag_ref[t] == 1)
            def _():
                _tile_fill(zero_ref, xs_ref, zsem, t, tm).wait()
            return carry

        lax.fori_loop(0, n_tiles, fill_wait, 0)

    def issue(r, carry):
        for k in range(TOP_K):
            _row_copy_scatter(xn_ref, xs_ref, sem, r, pos_ref[0, k * ts + r]).start(priority=k % 2)
        return carry

    lax.fori_loop(0, ts, issue, 0, unroll=ROW_COPY_UNROLL)

    def drain(r, carry):
        for k in range(TOP_K):
            _row_copy_scatter(xn_ref, xs_ref, sem, 0, 0).wait()
        return carry

    lax.fori_loop(0, ts, drain, 0, unroll=ROW_COPY_UNROLL)


def _tile_major(pos, ts):
    K, T = pos.shape
    return pos.reshape(K, T // ts, ts).transpose(1, 0, 2).reshape(T // ts, 1, K * ts)


def _dispatch(zflag, pos, h1, g2, n_tiles, tm, ts):
    T, D = h1.shape
    kern = functools.partial(_dispatch_kernel, ts=ts, tm=tm, n_tiles=n_tiles)
    grid_spec = pltpu.PrefetchScalarGridSpec(
        num_scalar_prefetch=1,
        grid=(T // ts,),
        in_specs=[
            pl.BlockSpec((None, 1, TOP_K * ts), lambda i, zf: (i, 0, 0), memory_space=pltpu.SMEM),
            pl.BlockSpec((ts, D), lambda i, zf: (i, 0)),
            pl.BlockSpec((1, D), lambda i, zf: (0, 0)),
        ],
        out_specs=pl.BlockSpec(memory_space=pl.ANY),
        scratch_shapes=[pltpu.VMEM((ts, D), F32), pltpu.VMEM((tm, D), F32),
                        pltpu.SemaphoreType.DMA, pltpu.SemaphoreType.DMA],
    )
    return pl.pallas_call(
        kern,
        out_shape=jax.ShapeDtypeStruct((n_tiles * tm, D), F32),
        grid_spec=grid_spec,
        compiler_params=_cparams("arbitrary"),
        name="dispatch",
    )(zflag, _tile_major(pos, ts), h1, g2)


def _row_copy_gather(src_ref, dst_ref, sem, slot, k, r, pos):
    return pltpu.make_async_copy(src_ref.at[pl.ds(pos, 1)], dst_ref.at[slot, k, pl.ds(r, 1)], sem.at[slot])


def _combine_kernel(pos_ref, nxt_ref, w_ref, h1_ref, gf_ref, y_ref, o_ref, buf_ref, sem, *, ts):
    i = pl.program_id(0)
    slot = i & 1

    def gather_tile(table_ref, dst_slot):
        def issue(r, carry):
            for k in range(TOP_K):
                _row_copy_gather(y_ref, buf_ref, sem, dst_slot, k, r, table_ref[0, k * ts + r]).start(
                    priority=k % 2)
            return carry

        lax.fori_loop(0, ts, issue, 0, unroll=ROW_COPY_UNROLL)

    @pl.when(i == 0)
    def _():
        gather_tile(pos_ref, 0)

    @pl.when(i + 1 < pl.num_programs(0))
    def _():
        gather_tile(nxt_ref, 1 - slot)

    def drain(r, carry):
        for k in range(TOP_K):
            _row_copy_gather(y_ref, buf_ref, sem, slot, k, 0, 0).wait()
        return carry

    lax.fori_loop(0, ts, drain, 0, unroll=ROW_COPY_UNROLL)
    acc = h1_ref[...]
    for k in range(TOP_K):
        acc = acc + w_ref[:, k:k + 1] * buf_ref[slot, k]
    o_ref[...] = _rms(acc, gf_ref[...])


def _combine(pos, w_tok, h1, gf, y, ts):
    T, D = h1.shape
    n = T // ts
    kern = functools.partial(_combine_kernel, ts=ts)
    pos_tiles = _tile_major(pos, ts)
    return pl.pallas_call(
        kern,
        out_shape=jax.ShapeDtypeStruct((T, D), F32),
        grid=(n,),
        in_specs=[
            pl.BlockSpec((None, 1, TOP_K * ts), lambda i: (i, 0, 0), memory_space=pltpu.SMEM),
            pl.BlockSpec((None, 1, TOP_K * ts), lambda i: (jnp.minimum(i + 1, n - 1), 0, 0),
                         memory_space=pltpu.SMEM),
            pl.BlockSpec((ts, TOP_K), lambda i: (i, 0)),
            pl.BlockSpec((ts, D), lambda i: (i, 0)),
            pl.BlockSpec((1, D), lambda i: (0, 0)),
            pl.BlockSpec(memory_space=pl.ANY),
        ],
        out_specs=pl.BlockSpec((ts, D), lambda i: (i, 0)),
        scratch_shapes=[pltpu.VMEM((2, TOP_K, ts, D), F32), pltpu.SemaphoreType.DMA((2,))],
        compiler_params=_cparams("arbitrary"),
        name="combine",
    )(pos_tiles, pos_tiles, w_tok, h1, gf, y)


TILE_READ_PRIORITY = 0
WEIGHT_READ_PRIORITY = 1


def _tile_read(in_hbm, in_buf, in_sem, slot, t, tm):
    return pltpu.make_async_copy(in_hbm.at[pl.ds(pl.multiple_of(t * tm, tm), tm)], in_buf.at[slot],
                                 in_sem.at[slot])


def _stream_expert_weights(copies, n_steps):
    n = pl.program_id(0)
    e = pl.program_id(1)
    E = pl.num_programs(1)
    s = n * E + e
    slot = s & 1

    @pl.when(s == 0)
    def _():
        for c in copies(0, 0, 0):
            c.start(priority=WEIGHT_READ_PRIORITY)

    @pl.when(s + 1 < n_steps)
    def _():
        wrap = e + 1 == E
        for c in copies(1 - slot, jnp.where(wrap, n + 1, n), jnp.where(wrap, 0, e + 1)):
            c.start(priority=WEIGHT_READ_PRIORITY)

    for c in copies(slot, n, e):
        c.wait()
    return slot


def _group_tile_loop(t0, nt, used, n_tiles, last_group, in_hbm, out_hbm, in_buf, out_buf, in_sem, out_sem,
                     col0, tm, tn, compute):
    def in_copy(slot, t):
        return _tile_read(in_hbm, in_buf, in_sem, slot, t, tm)

    def out_copy(slot, t):
        dst = out_hbm.at[pl.ds(pl.multiple_of(t * tm, tm), tm), pl.ds(col0, tn)]
        return pltpu.make_async_copy(out_buf.at[slot], dst, out_sem.at[slot])

    def body(j, carry):
        slot = j & 1
        in_copy(slot, t0 + j).wait()

        @pl.when(j + 1 < nt)
        def _():
            in_copy(1 - slot, t0 + j + 1).start(priority=TILE_READ_PRIORITY)

        res = compute(in_buf[slot])

        @pl.when(j >= 2)
        def _():
            out_copy(slot, t0 + j - 2).wait()

        out_buf[slot] = res
        out_copy(slot, t0 + j).start()
        return carry

    lax.fori_loop(0, nt, body, 0)

    @pl.when(nt >= 2)
    def _():
        out_copy(nt & 1, t0 + nt - 2).wait()

    @pl.when(nt >= 1)
    def _():
        out_copy((nt - 1) & 1, t0 + nt - 1).wait()

    @pl.when(last_group)
    def _():
        out_buf[0] = jnp.zeros(out_buf.shape[1:], out_buf.dtype)

        def fill(t, carry):
            out_copy(0, t).start()
            out_copy(0, t).wait()
            return carry

        lax.fori_loop(used, n_tiles, fill, 0)


def _gmm1_kernel(t0_ref, nt_ref, ba_ref, bb_ref, w_ref, xs_ref, o_ref,
                 w_raw, w_sem, wa_s, wb_s, in_buf, out_buf, in_sem, out_sem, *, tm, tn, n_tiles):
    n = pl.program_id(0)
    e = pl.program_id(1)
    E = pl.num_programs(1)
    nb = pl.num_programs(0)
    t0 = t0_ref[e]
    nt = nt_ref[e]

    @pl.when(nt > 0)
    def _():
        _tile_read(xs_ref, in_buf, in_sem, 0, t0, tm).start(priority=TILE_READ_PRIORITY)

    def weight_copies(slot, n_, e_):
        return [pltpu.make_async_copy(w_ref.at[e_, :, pl.ds(pl.multiple_of((h * nb + n_) * tn, tn), tn)],
                                      w_raw.at[slot, h], w_sem.at[slot, h]) for h in range(2)]

    slot = _stream_expert_weights(weight_copies, nb * E)

    @pl.when(nt > 0)
    def _():
        wa_s[...] = w_raw[slot, 0].astype(BF16)
        wb_s[...] = w_raw[slot, 1].astype(BF16)

    def compute(xt):
        x = xt.astype(BF16)
        hg = jnp.dot(x, wa_s[...], preferred_element_type=F32) + ba_ref[...]
        hl = jnp.dot(x, wb_s[...], preferred_element_type=F32) + bb_ref[...]
        hg = jnp.minimum(hg, SWIGLU_LIMIT)
        hl = jnp.clip(hl, -SWIGLU_LIMIT, SWIGLU_LIMIT)
        return (hg * jax.nn.sigmoid(SWIGLU_ALPHA * hg) * (hl + 1.0)).astype(BF16)

    used = t0_ref[E - 1] + nt_ref[E - 1]
    _group_tile_loop(t0, nt, used, n_tiles, e == E - 1, xs_ref, o_ref, in_buf, out_buf, in_sem, out_sem,
                     pl.multiple_of(n * tn, tn), tm, tn, compute)


def _gmm1(t0, nt, xs, w1, b1, tm, tn):
    rows = xs.shape[0]
    E, D, F2 = w1.shape
    F = F2 // 2
    nb = F // tn
    b1r = b1.reshape(E, 1, F2)
    kern = functools.partial(_gmm1_kernel, tm=tm, tn=tn, n_tiles=rows // tm)
    grid_spec = pltpu.PrefetchScalarGridSpec(
        num_scalar_prefetch=2,
        grid=(nb, E),
        in_specs=[
            pl.BlockSpec((None, 1, tn), lambda n, e, t0, nt: (e, 0, n)),
            pl.BlockSpec((None, 1, tn), lambda n, e, t0, nt: (e, 0, nb + n)),
            pl.BlockSpec(memory_space=pl.ANY),
            pl.BlockSpec(memory_space=pl.ANY),
        ],
        out_specs=pl.BlockSpec(memory_space=pl.ANY),
        scratch_shapes=[
            pltpu.VMEM((2, 2, D, tn), F32), pltpu.SemaphoreType.DMA((2, 2)),
            pltpu.VMEM((D, tn), BF16), pltpu.VMEM((D, tn), BF16),
            pltpu.VMEM((2, tm, D), F32), pltpu.VMEM((2, tm, tn), BF16),
            pltpu.SemaphoreType.DMA((2,)), pltpu.SemaphoreType.DMA((2,)),
        ],
    )
    return pl.pallas_call(
        kern,
        out_shape=jax.ShapeDtypeStruct((rows, F), BF16),
        grid_spec=grid_spec,
        compiler_params=_cparams("arbitrary", "arbitrary"),
        name="gmm1_swiglu",
    )(t0, nt, b1r, b1r, w1, xs)


def _gmm2_kernel(t0_ref, nt_ref, b_ref, w_ref, a_ref, o_ref, w_raw, w_sem, w_s, in_buf, out_buf,
                 in_sem, out_sem, *, tm, tn, n_tiles):
    n = pl.program_id(0)
    e = pl.program_id(1)
    E = pl.num_programs(1)
    t0 = t0_ref[e]
    nt = nt_ref[e]

    @pl.when(nt > 0)
    def _():
        _tile_read(a_ref, in_buf, in_sem, 0, t0, tm).start(priority=TILE_READ_PRIORITY)

    def weight_copies(slot, n_, e_):
        return [pltpu.make_async_copy(w_ref.at[e_, :, pl.ds(pl.multiple_of(n_ * tn, tn), tn)],
                                      w_raw.at[slot], w_sem.at[slot])]

    slot = _stream_expert_weights(weight_copies, pl.num_programs(0) * E)

    @pl.when(nt > 0)
    def _():
        w_s[...] = w_raw[slot].astype(BF16)

    def compute(a):
        return jnp.dot(a, w_s[...], preferred_element_type=F32) + b_ref[...]

    used = t0_ref[E - 1] + nt_ref[E - 1]
    _group_tile_loop(t0, nt, used, n_tiles, e == E - 1, a_ref, o_ref, in_buf, out_buf, in_sem, out_sem,
                     pl.multiple_of(n * tn, tn), tm, tn, compute)


def _gmm2(t0, nt, act, w2, b2, tm, tn):
    rows, F = act.shape
    E, _, D = w2.shape
    b2r = b2.reshape(E, 1, D)
    kern = functools.partial(_gmm2_kernel, tm=tm, tn=tn, n_tiles=rows // tm)
    grid_spec = pltpu.PrefetchScalarGridSpec(
        num_scalar_prefetch=2,
        grid=(D // tn, E),
        in_specs=[
            pl.BlockSpec((None, 1, tn), lambda n, e, t0, nt: (e, 0, n)),
            pl.BlockSpec(memory_space=pl.ANY),
            pl.BlockSpec(memory_space=pl.ANY),
        ],
        out_specs=pl.BlockSpec(memory_space=pl.ANY),
        scratch_shapes=[
            pltpu.VMEM((2, F, tn), F32), pltpu.SemaphoreType.DMA((2,)),
            pltpu.VMEM((F, tn), BF16),
            pltpu.VMEM((2, tm, F), BF16), pltpu.VMEM((2, tm, tn), F32),
            pltpu.SemaphoreType.DMA((2,)), pltpu.SemaphoreType.DMA((2,)),
        ],
    )
    return pl.pallas_call(
        kern,
        out_shape=jax.ShapeDtypeStruct((rows, D), F32),
        grid_spec=grid_spec,
        compiler_params=_cparams("arbitrary", "arbitrary"),
        name="gmm2",
    )(t0, nt, b2r, w2, act)


def _tile_plan(counts, n_tiles, tm):
    ptiles = (counts + tm - 1) // tm
    ends = jnp.cumsum(ptiles)
    starts = ends - ptiles
    t = jnp.arange(n_tiles, dtype=I32)[:, None]
    last_of_group = jnp.any((ptiles[None, :] > 0) & (t == ends[None, :] - 1), axis=1)
    zflag = (last_of_group | (t[:, 0] >= ends[-1])).astype(I32)
    return starts.astype(I32), ptiles.astype(I32), (starts * tm).astype(I32), zflag


def kernel(x, norm_mix_g, w_in, conv_w, conv_b, conv_ln_g, conv_ln_b, w_conv_out, pool_w, pool_scale,
           w_pool_out, w_o, norm_ffn_g, w_router, b_router, w1, b1, w2, b2, norm_final_g):
    B, S, D = x.shape
    T = B * S
    assert w_in.shape[0] == 1, "the combine kernel fuses the final norm: one block only"
    E = w_router.shape[2]
    F = w2.shape[2]
    C = w_conv_out.shape[1]
    h = x.reshape(T, D)
    w_in_bf = w_in[0].astype(BF16)
    g1 = norm_mix_g[0].reshape(1, D)
    u, p, sg = _inproj(h, g1, w_in_bf, C, _tile(T, 1024), _tile(C, 512))
    ts_seq = _tile(S, 512)
    c = _conv_branch(u.reshape(B, S, C), conv_w[0], conv_b[0], conv_ln_g[0], conv_ln_b[0], ts_seq)
    q = _pool_branch(p.reshape(B, S, C), pool_w[0].astype(BF16), pool_scale[0], ts_seq)
    g2 = norm_ffn_g[0].reshape(1, D)
    h1 = _merge(c.reshape(T, C), q.reshape(T, C), sg, h, w_conv_out[0].astype(BF16),
                w_pool_out[0].astype(BF16), w_o[0].astype(BF16), _tile(T, 256))
    wr_pad = jnp.pad(w_router[0], ((0, 0), (0, V7X_LANES - E)))
    idx_t, w_t, rank_t, cnt = _router(h1, g2, wr_pad, b_router[0].reshape(E, 1), E, _tile(T, 512))
    tm = _tile(T, 256)
    n_tiles = (T * TOP_K) // tm + E
    t0, nt, row_start, zflag = _tile_plan(cnt[:, 0].astype(I32), n_tiles, tm)
    pos = _positions(row_start, idx_t, rank_t, _tile(T, 2048))
    xs = _dispatch(zflag, pos, h1, g2, n_tiles, tm, _tile(T, 512))
    act = _gmm1(t0, nt, xs, w1[0], b1[0], tm, _tile(F, 1024))
    y = _gmm2(t0, nt, act, w2[0], b2[0], tm, _tile(D, 2048))
    out = _combine(pos, w_t.T, h1, norm_final_g.reshape(1, D), y, _tile(T, 256))
    return out.reshape(B, S, D)
```

```python
import functools

import jax
import jax.numpy as jnp
from jax import lax
from jax.experimental import pallas as pl
from jax.experimental.pallas import tpu as pltpu

NORM_EPS = 1e-5
POOL_WINDOWS = (2, 4, 8, 16)
TOP_K = 4
SWIGLU_ALPHA = 1.702
SWIGLU_LIMIT = 7.0

V7X_LANES = 128
V7X_SUBLANES = 8
V7X_VMEM_BYTES = 64 * 1024 * 1024
VMEM_LIMIT_BYTES = V7X_VMEM_BYTES - 8 * 1024 * 1024

F32 = jnp.float32
BF16 = jnp.bfloat16
I32 = jnp.int32


def _cparams(*sem):
    return pltpu.CompilerParams(dimension_semantics=sem, vmem_limit_bytes=VMEM_LIMIT_BYTES)


def _tile(n, pref):
    t = min(n, pref)
    while n % t:
        t //= 2
    return t


def _rms(x, g):
    ms = jnp.mean(x * x, axis=-1, keepdims=True)
    return x * lax.rsqrt(ms + NORM_EPS) * g


INPROJ_ROW_CHUNKS = 4


def _inproj_kernel(x_ref, g_ref, wa_ref, wb_ref, u_ref, p_ref, sg_ref, xn_ref, *, nu, npool):
    j = pl.program_id(1)

    @pl.when(j == 0)
    def _():
        xn_ref[...] = _rms(x_ref[...], g_ref[...]).astype(BF16)

    tm = x_ref.shape[0]
    rc = tm // INPROJ_ROW_CHUNKS
    chunks = [slice(c * rc, (c + 1) * rc) for c in range(INPROJ_ROW_CHUNKS)]

    def proj(w_ref, rows):
        return jnp.dot(xn_ref[rows, :], w_ref[...], preferred_element_type=F32)

    @pl.when(j < nu)
    def _():
        for rows in chunks:
            u_ref[rows, :] = (proj(wa_ref, rows) * jax.nn.sigmoid(proj(wb_ref, rows))).astype(u_ref.dtype)

    @pl.when((j >= nu) & (j < nu + npool))
    def _():
        for rows in chunks:
            p_ref[rows, :] = proj(wa_ref, rows)

    @pl.when(j >= nu + npool)
    def _():
        for rows in chunks:
            sg_ref[rows, :] = jax.nn.sigmoid(proj(wa_ref, rows)).astype(sg_ref.dtype)


def _inproj(x2, g, w_bf, C, tm, tn):
    T, D = x2.shape
    nu = C // tn
    npool = C // tn
    nsg = (w_bf.shape[1] - 3 * C) // tn
    kern = functools.partial(_inproj_kernel, nu=nu, npool=npool)
    return pl.pallas_call(
        kern,
        out_shape=(jax.ShapeDtypeStruct((T, C), BF16), jax.ShapeDtypeStruct((T, C), F32),
                   jax.ShapeDtypeStruct((T, nsg * tn), BF16)),
        grid=(T // tm, nu + npool + nsg),
        in_specs=[
            pl.BlockSpec((tm, D), lambda i, j: (i, 0)),
            pl.BlockSpec((1, D), lambda i, j: (0, 0)),
            pl.BlockSpec((D, tn), lambda i, j: (0, jnp.where(j < nu, j, j + nu))),
            pl.BlockSpec((D, tn), lambda i, j: (0, nu + jnp.minimum(j, nu - 1))),
        ],
        out_specs=(
            pl.BlockSpec((tm, tn), lambda i, j: (i, jnp.minimum(j, nu - 1))),
            pl.BlockSpec((tm, tn), lambda i, j: (i, jnp.clip(j - nu, 0, npool - 1))),
            pl.BlockSpec((tm, tn), lambda i, j: (i, jnp.clip(j - nu - npool, 0, nsg - 1))),
        ),
        scratch_shapes=[pltpu.VMEM((tm, D), BF16)],
        compiler_params=_cparams("parallel", "arbitrary"),
        name="inproj",
    )(x2, g, w_bf, w_bf)


CONV_HALO = 32
CONV_ROWS = 32
CONV_LANES = V7X_LANES


def _conv_kernel(cur_ref, halo_ref, w_ref, b_ref, g_ref, beta_ref, o_ref, ext_ref, *, ts, C, KW):
    s = pl.program_id(1)
    halo = halo_ref[...].astype(F32)
    ext_ref[0:CONV_HALO, :] = jnp.where(s > 0, halo, 0.0)
    ext_ref[CONV_HALO:, :] = cur_ref[...].astype(F32)
    off = CONV_HALO - (KW - 1)
    lc = min(C, CONV_LANES)
    sub = V7X_SUBLANES
    region_rows = CONV_ROWS + CONV_HALO

    def body(i, carry):
        r0 = pl.multiple_of(i * CONV_ROWS, CONV_ROWS)
        parts = []
        for c0 in range(0, C, lc):
            region = ext_ref[pl.ds(r0, region_rows), c0:c0 + lc]
            acc = jnp.zeros((CONV_ROWS, lc), F32) + b_ref[:, c0:c0 + lc]
            for r in range(sub):
                span = CONV_ROWS + (sub if r else 0)
                part = None
                for a in range(CONV_HALO // sub + 1):
                    k = sub * a + r - off
                    if 0 <= k < KW:
                        assert sub * a + span <= region_rows
                        term = region[sub * a:sub * a + span, :] * w_ref[k:k + 1, c0:c0 + lc]
                        part = term if part is None else part + term
                if part is not None:
                    acc = acc + part[r:r + CONV_ROWS, :]
            parts.append(acc)
        y = jnp.concatenate(parts, axis=1) if len(parts) > 1 else parts[0]
        mu = jnp.mean(y, axis=-1, keepdims=True)
        d = y - mu
        var = jnp.mean(d * d, axis=-1, keepdims=True)
        yn = d * lax.rsqrt(var + NORM_EPS) * g_ref[...] + beta_ref[...]
        o_ref[pl.ds(r0, CONV_ROWS), :] = (yn * jax.nn.sigmoid(yn)).astype(o_ref.dtype)
        return carry

    lax.fori_loop(0, ts // CONV_ROWS, body, 0)


def _conv_branch(u3, conv_w, conv_b, ln_g, ln_b, ts):
    B, S, C = u3.shape
    KW = conv_w.shape[0]
    assert KW - 1 <= CONV_HALO and ts % CONV_HALO == 0
    hb = ts // CONV_HALO
    kern = functools.partial(_conv_kernel, ts=ts, C=C, KW=KW)
    vec = lambda: pl.BlockSpec((1, C), lambda b, s: (0, 0))
    return pl.pallas_call(
        kern,
        out_shape=jax.ShapeDtypeStruct((B, S, C), BF16),
        grid=(B, S // ts),
        in_specs=[
            pl.BlockSpec((None, ts, C), lambda b, s: (b, s, 0)),
            pl.BlockSpec((None, CONV_HALO, C), lambda b, s: (b, jnp.maximum(s * hb - 1, 0), 0)),
            pl.BlockSpec((KW, C), lambda b, s: (0, 0)),
            vec(), vec(), vec(),
        ],
        out_specs=pl.BlockSpec((None, ts, C), lambda b, s: (b, s, 0)),
        scratch_shapes=[pltpu.VMEM((CONV_HALO + ts, C), F32)],
        compiler_params=_cparams("parallel", "arbitrary"),
        name="conv_branch",
    )(u3, u3, conv_w, conv_b.reshape(1, C), ln_g.reshape(1, C), ln_b.reshape(1, C))


POOL_HALO = 16
POOL_ROWS = 32


def _pool_kernel(cur_ref, halo_ref, pw_ref, ps_ref, o_ref, ext_ref, diff_ref, *, ts, C):
    s = pl.program_id(1)
    G = len(POOL_WINDOWS)
    cg = C // G
    ext_ref[0:POOL_HALO, :] = jnp.where(s > 0, halo_ref[...], 0.0)
    ext_ref[POOL_HALO:, :] = cur_ref[...]

    def body(i, carry):
        r0 = pl.multiple_of(i * POOL_ROWS, POOL_ROWS)
        row = lax.broadcasted_iota(I32, (POOL_ROWS, 1), 0)
        pos1 = (s * ts + r0 + row + 1).astype(F32)
        for gi, w in enumerate(POOL_WINDOWS):
            win = ext_ref[pl.ds(r0, POOL_ROWS + POOL_HALO), gi * cg:(gi + 1) * cg]
            tok = win[POOL_HALO:POOL_HALO + POOL_ROWS, :]
            tot = tok
            for j in range(1, w):
                tot = tot + win[POOL_HALO - j:POOL_HALO - j + POOL_ROWS, :]
            d = tot / jnp.minimum(pos1, float(w)) - tok
            diff_ref[pl.ds(r0, POOL_ROWS), gi * cg:(gi + 1) * cg] = d.astype(BF16)
        return carry

    lax.fori_loop(0, ts // POOL_ROWS, body, 0)
    for gi in range(G):
        y = jnp.dot(diff_ref[:, gi * cg:(gi + 1) * cg], pw_ref[gi], preferred_element_type=F32)
        o_ref[:, gi * cg:(gi + 1) * cg] = (y * ps_ref[:, gi * cg:(gi + 1) * cg]).astype(o_ref.dtype)


def _pool_branch(p3, pool_w_bf, pool_scale, ts):
    B, S, C = p3.shape
    G, cg, _ = pool_w_bf.shape
    assert G == len(POOL_WINDOWS) and max(POOL_WINDOWS) - 1 <= POOL_HALO and ts % POOL_ROWS == 0
    hb = ts // POOL_HALO
    kern = functools.partial(_pool_kernel, ts=ts, C=C)
    return pl.pallas_call(
        kern,
        out_shape=jax.ShapeDtypeStruct((B, S, C), BF16),
        grid=(B, S // ts),
        in_specs=[
            pl.BlockSpec((None, ts, C), lambda b, s: (b, s, 0)),
            pl.BlockSpec((None, POOL_HALO, C), lambda b, s: (b, jnp.maximum(s * hb - 1, 0), 0)),
            pl.BlockSpec((G, cg, cg), lambda b, s: (0, 0, 0)),
            pl.BlockSpec((1, C), lambda b, s: (0, 0)),
        ],
        out_specs=pl.BlockSpec((None, ts, C), lambda b, s: (b, s, 0)),
        scratch_shapes=[pltpu.VMEM((POOL_HALO + ts, C), F32), pltpu.VMEM((ts, C), BF16)],
        compiler_params=_cparams("parallel", "arbitrary"),
        name="pool_branch",
    )(p3, p3, pool_w_bf, pool_scale.reshape(1, C))


def _merge_kernel(c_ref, q_ref, sg_ref, x_ref, wco_ref, wpo_ref, wo_ref, h1_ref):
    D = x_ref.shape[1]
    yc = jnp.dot(c_ref[...], wco_ref[...], preferred_element_type=F32)
    yp = jnp.dot(q_ref[...], wpo_ref[...], preferred_element_type=F32)
    merged = sg_ref[:, :D].astype(F32) * yc + sg_ref[:, D:].astype(F32) * yp
    h1_ref[...] = x_ref[...] + jnp.dot(merged.astype(BF16), wo_ref[...], preferred_element_type=F32)


def _merge(c2, q2, sg, x2, wco, wpo, wo, tm):
    T, D = x2.shape
    C = c2.shape[1]
    full = lambda shape: pl.BlockSpec(shape, lambda i: (0, 0), pipeline_mode=pl.Buffered(1))
    return pl.pallas_call(
        _merge_kernel,
        out_shape=jax.ShapeDtypeStruct((T, D), F32),
        grid=(T // tm,),
        in_specs=[
            pl.BlockSpec((tm, C), lambda i: (i, 0)),
            pl.BlockSpec((tm, C), lambda i: (i, 0)),
            pl.BlockSpec((tm, 2 * D), lambda i: (i, 0)),
            pl.BlockSpec((tm, D), lambda i: (i, 0)),
            full((C, D)), full((C, D)), full((D, D)),
        ],
        out_specs=pl.BlockSpec((tm, D), lambda i: (i, 0)),
        compiler_params=_cparams("parallel"),
        name="merge_outproj",
    )(c2, q2, sg, x2, wco, wpo, wo)


def _router_kernel(h1_ref, g2_ref, wr_ref, br_ref, idx_ref, w_ref, rank_ref, cnt_ref, carry_ref, *, E, ts):
    i = pl.program_id(0)

    @pl.when(i == 0)
    def _():
        carry_ref[...] = jnp.zeros_like(carry_ref)

    xn = _rms(h1_ref[...], g2_ref[...])
    x_hi = xn.astype(BF16)
    x_lo = (xn - x_hi.astype(F32)).astype(BF16)
    wr = wr_ref[...]
    w_hi = wr.astype(BF16)
    w_lo = (wr - w_hi.astype(F32)).astype(BF16)
    lg = (jnp.dot(x_hi, w_hi, preferred_element_type=F32)
          + jnp.dot(x_hi, w_lo, preferred_element_type=F32)
          + jnp.dot(x_lo, w_hi, preferred_element_type=F32))
    vals = lg.T[:E, :] + br_ref[...]
    e_iota = lax.broadcasted_iota(I32, (E, ts), 0)
    tops, idxs, hots = [], [], []
    for _ in range(TOP_K):
        m = jnp.max(vals, axis=0, keepdims=True)
        idx = jnp.min(jnp.where(vals == m, e_iota, E), axis=0, keepdims=True)
        hot = e_iota == idx
        vals = jnp.where(hot, -jnp.inf, vals)
        tops.append(m)
        idxs.append(idx)
        hots.append(hot)
    exps = [jnp.exp(t - tops[0]) for t in tops]
    den = exps[0] + exps[1] + exps[2] + exps[3]
    w_ref[...] = jnp.concatenate([e / den for e in exps], axis=0)
    idx_ref[...] = jnp.concatenate(idxs, axis=0)

    mask = jnp.zeros((E, ts), F32)
    for hot in hots:
        mask = mask + hot.astype(F32)
    before = lax.broadcasted_iota(I32, (ts, ts), 0) < lax.broadcasted_iota(I32, (ts, ts), 1)
    excl = jnp.dot(mask.astype(BF16), before.astype(BF16), preferred_element_type=F32)
    base = excl + carry_ref[:, 0:1]
    ranks = [jnp.sum(jnp.where(hot, base, 0.0), axis=0, keepdims=True) for hot in hots]
    rank_ref[...] = jnp.concatenate(ranks, axis=0).astype(I32)
    carry_ref[...] = carry_ref[...] + jnp.sum(mask, axis=1, keepdims=True)
    cnt_ref[...] = carry_ref[...]


def _router(h1, g2, wr_pad, br_col, E, ts):
    T, D = h1.shape
    kern = functools.partial(_router_kernel, E=E, ts=ts)
    tok = lambda: pl.BlockSpec((TOP_K, ts), lambda i: (0, i))
    return pl.pallas_call(
        kern,
        out_shape=(jax.ShapeDtypeStruct((TOP_K, T), I32), jax.ShapeDtypeStruct((TOP_K, T), F32),
                   jax.ShapeDtypeStruct((TOP_K, T), I32), jax.ShapeDtypeStruct((E, V7X_LANES), F32)),
        grid=(T // ts,),
        in_specs=[
            pl.BlockSpec((ts, D), lambda i: (i, 0)),
            pl.BlockSpec((1, D), lambda i: (0, 0)),
            pl.BlockSpec((D, V7X_LANES), lambda i: (0, 0)),
            pl.BlockSpec((E, 1), lambda i: (0, 0)),
        ],
        out_specs=(tok(), tok(), tok(), pl.BlockSpec((E, V7X_LANES), lambda i: (0, 0))),
        scratch_shapes=[pltpu.VMEM((E, V7X_LANES), F32)],
        compiler_params=_cparams("arbitrary"),
        name="router",
    )(h1, g2, wr_pad, br_col)


def _pos_kernel(start_ref, idx_ref, rank_ref, pos_ref, *, E):
    idx = idx_ref[...]
    base = jnp.zeros_like(idx)
    for e in range(E):
        base = jnp.where(idx == e, start_ref[e], base)
    pos_ref[...] = base + rank_ref[...]


def _positions(row_start, idx_t, rank_t, tl):
    K, T = idx_t.shape
    E = row_start.shape[0]
    tok = lambda: pl.BlockSpec((K, tl), lambda i: (0, i))
    return pl.pallas_call(
        functools.partial(_pos_kernel, E=E),
        out_shape=jax.ShapeDtypeStruct((K, T), I32),
        grid=(T // tl,),
        in_specs=[pl.BlockSpec(memory_space=pltpu.SMEM), tok(), tok()],
        out_specs=tok(),
        compiler_params=_cparams("parallel"),
        name="positions",
    )(row_start, idx_t, rank_t)


ROW_COPY_UNROLL = 4


def _row_copy_scatter(src_ref, dst_ref, sem, r, pos):
    return pltpu.make_async_copy(src_ref.at[pl.ds(r, 1)], dst_ref.at[pl.ds(pos, 1)], sem)


def _tile_fill(zero_ref, dst_ref, sem, t, tm):
    return pltpu.make_async_copy(zero_ref, dst_ref.at[pl.ds(pl.multiple_of(t * tm, tm), tm)], sem)


def _dispatch_kernel(zflag_ref, pos_ref, h1_ref, g2_ref, xs_ref, xn_ref, zero_ref, sem, zsem,
                     *, ts, tm, n_tiles):
    i = pl.program_id(0)
    slot = i & 1
    xn_ref[slot] = _rms(h1_ref[...], g2_ref[...])

    @pl.when(i == 0)
    def _():
        zero_ref[...] = jnp.zeros_like(zero_ref)

        def fill(t, carry):
            @pl.when(zflag_ref[t] == 1)
            def _():
                _tile_fill(zero_ref, xs_ref, zsem, t, tm).start()
            return carry

        lax.fori_loop(0, n_tiles, fill, 0)

        def fill_wait(t, carry):
            @pl.when(zflag_ref[t] == 1)
            def _():
                _tile_fill(zero_ref, xs_ref, zsem, t, tm).wait()
            return carry

        lax.fori_loop(0, n_tiles, fill_wait, 0)

    def issue(r, carry):
        for k in range(TOP_K):
            _row_copy_scatter(xn_ref.at[slot], xs_ref, sem.at[slot], r, pos_ref[0, k * ts + r]).start(
                priority=k % 2)
        return carry

    lax.fori_loop(0, ts, issue, 0, unroll=ROW_COPY_UNROLL)

    def drain_tile(s):
        def drain(r, carry):
            for k in range(TOP_K):
                _row_copy_scatter(xn_ref.at[s], xs_ref, sem.at[s], 0, 0).wait()
            return carry

        lax.fori_loop(0, ts, drain, 0, unroll=ROW_COPY_UNROLL)

    @pl.when(i > 0)
    def _():
        drain_tile(1 - slot)

    @pl.when(i == pl.num_programs(0) - 1)
    def _():
        drain_tile(slot)


def _tile_major(pos, ts):
    K, T = pos.shape
    return pos.reshape(K, T // ts, ts).transpose(1, 0, 2).reshape(T // ts, 1, K * ts)


def _dispatch(zflag, pos, h1, g2, n_tiles, tm, ts):
    T, D = h1.shape
    kern = functools.partial(_dispatch_kernel, ts=ts, tm=tm, n_tiles=n_tiles)
    grid_spec = pltpu.PrefetchScalarGridSpec(
        num_scalar_prefetch=1,
        grid=(T // ts,),
        in_specs=[
            pl.BlockSpec((None, 1, TOP_K * ts), lambda i, zf: (i, 0, 0), memory_space=pltpu.SMEM),
            pl.BlockSpec((ts, D), lambda i, zf: (i, 0)),
            pl.BlockSpec((1, D), lambda i, zf: (0, 0)),
        ],
        out_specs=pl.BlockSpec(memory_space=pl.ANY),
        scratch_shapes=[pltpu.VMEM((2, ts, D), F32), pltpu.VMEM((tm, D), F32),
                        pltpu.SemaphoreType.DMA((2,)), pltpu.SemaphoreType.DMA],
    )
    return pl.pallas_call(
        kern,
        out_shape=jax.ShapeDtypeStruct((n_tiles * tm, D), F32),
        grid_spec=grid_spec,
        compiler_params=_cparams("arbitrary"),
        name="dispatch",
    )(zflag, _tile_major(pos, ts), h1, g2)


def _row_copy_gather(src_ref, dst_ref, sem, slot, k, r, pos):
    return pltpu.make_async_copy(src_ref.at[pl.ds(pos, 1)], dst_ref.at[slot, k, pl.ds(r, 1)], sem.at[slot])


def _combine_kernel(pos_ref, nxt_ref, w_ref, h1_ref, gf_ref, y_ref, o_ref, buf_ref, sem, *, ts):
    i = pl.program_id(0)
    slot = i & 1

    def gather_tile(table_ref, dst_slot):
        def issue(r, carry):
            for k in range(TOP_K):
                _row_copy_gather(y_ref, buf_ref, sem, dst_slot, k, r, table_ref[0, k * ts + r]).start(
                    priority=k % 2)
            return carry

        lax.fori_loop(0, ts, issue, 0, unroll=ROW_COPY_UNROLL)

    @pl.when(i == 0)
    def _():
        gather_tile(pos_ref, 0)

    @pl.when(i + 1 < pl.num_programs(0))
    def _():
        gather_tile(nxt_ref, 1 - slot)

    def drain(r, carry):
        for k in range(TOP_K):
            _row_copy_gather(y_ref, buf_ref, sem, slot, k, 0, 0).wait()
        return carry

    lax.fori_loop(0, ts, drain, 0, unroll=ROW_COPY_UNROLL)
    acc = h1_ref[...]
    for k in range(TOP_K):
        acc = acc + w_ref[:, k:k + 1] * buf_ref[slot, k]
    o_ref[...] = _rms(acc, gf_ref[...])


def _combine(pos, w_tok, h1, gf, y, ts):
    T, D = h1.shape
    n = T // ts
    kern = functools.partial(_combine_kernel, ts=ts)
    pos_tiles = _tile_major(pos, ts)
    return pl.pallas_call(
        kern,
        out_shape=jax.ShapeDtypeStruct((T, D), F32),
        grid=(n,),
        in_specs=[
            pl.BlockSpec((None, 1, TOP_K * ts), lambda i: (i, 0, 0), memory_space=pltpu.SMEM),
            pl.BlockSpec((None, 1, TOP_K * ts), lambda i: (jnp.minimum(i + 1, n - 1), 0, 0),
                         memory_space=pltpu.SMEM),
            pl.BlockSpec((ts, TOP_K), lambda i: (i, 0)),
            pl.BlockSpec((ts, D), lambda i: (i, 0)),
            pl.BlockSpec((1, D), lambda i: (0, 0)),
            pl.BlockSpec(memory_space=pl.ANY),
        ],
        out_specs=pl.BlockSpec((ts, D), lambda i: (i, 0)),
        scratch_shapes=[pltpu.VMEM((2, TOP_K, ts, D), F32), pltpu.SemaphoreType.DMA((2,))],
        compiler_params=_cparams("arbitrary"),
        name="combine",
    )(pos_tiles, pos_tiles, w_tok, h1, gf, y)


TILE_READ_PRIORITY = 0
WEIGHT_READ_PRIORITY = 1


def _tile_read(in_hbm, in_buf, in_sem, slot, t, tm):
    return pltpu.make_async_copy(in_hbm.at[pl.ds(pl.multiple_of(t * tm, tm), tm)], in_buf.at[slot],
                                 in_sem.at[slot])


def _stream_expert_weights(copies, n_steps):
    n = pl.program_id(0)
    e = pl.program_id(1)
    E = pl.num_programs(1)
    s = n * E + e
    slot = s & 1

    @pl.when(s == 0)
    def _():
        for c in copies(0, 0, 0):
            c.start(priority=WEIGHT_READ_PRIORITY)

    @pl.when(s + 1 < n_steps)
    def _():
        wrap = e + 1 == E
        for c in copies(1 - slot, jnp.where(wrap, n + 1, n), jnp.where(wrap, 0, e + 1)):
            c.start(priority=WEIGHT_READ_PRIORITY)

    for c in copies(slot, n, e):
        c.wait()
    return slot


def _group_tile_loop(t0, nt, used, n_tiles, last_group, in_hbm, out_hbm, in_buf, out_buf, in_sem, out_sem,
                     col0, tm, tn, compute):
    def in_copy(slot, t):
        return _tile_read(in_hbm, in_buf, in_sem, slot, t, tm)

    def out_copy(slot, t):
        dst = out_hbm.at[pl.ds(pl.multiple_of(t * tm, tm), tm), pl.ds(col0, tn)]
        return pltpu.make_async_copy(out_buf.at[slot], dst, out_sem.at[slot])

    def body(j, carry):
        slot = j & 1
        in_copy(slot, t0 + j).wait()

        @pl.when(j + 1 < nt)
        def _():
            in_copy(1 - slot, t0 + j + 1).start(priority=TILE_READ_PRIORITY)

        res = compute(in_buf[slot])

        @pl.when(j >= 2)
        def _():
            out_copy(slot, t0 + j - 2).wait()

        out_buf[slot] = res
        out_copy(slot, t0 + j).start()
        return carry

    lax.fori_loop(0, nt, body, 0)

    @pl.when(nt >= 2)
    def _():
        out_copy(nt & 1, t0 + nt - 2).wait()

    @pl.when(nt >= 1)
    def _():
        out_copy((nt - 1) & 1, t0 + nt - 1).wait()

    @pl.when(last_group)
    def _():
        out_buf[0] = jnp.zeros(out_buf.shape[1:], out_buf.dtype)

        def fill(t, carry):
            out_copy(0, t).start()
            out_copy(0, t).wait()
            return carry

        lax.fori_loop(used, n_tiles, fill, 0)


def _gmm1_kernel(t0_ref, nt_ref, ba_ref, bb_ref, w_ref, xs_ref, o_ref,
                 w_raw, w_sem, wa_s, wb_s, in_buf, out_buf, in_sem, out_sem, *, tm, tn, n_tiles):
    n = pl.program_id(0)
    e = pl.program_id(1)
    E = pl.num_programs(1)
    nb = pl.num_programs(0)
    t0 = t0_ref[e]
    nt = nt_ref[e]

    @pl.when(nt > 0)
    def _():
        _tile_read(xs_ref, in_buf, in_sem, 0, t0, tm).start(priority=TILE_READ_PRIORITY)

    def weight_copies(slot, n_, e_):
        return [pltpu.make_async_copy(w_ref.at[e_, :, pl.ds(pl.multiple_of((h * nb + n_) * tn, tn), tn)],
                                      w_raw.at[slot, h], w_sem.at[slot, h]) for h in range(2)]

    slot = _stream_expert_weights(weight_copies, nb * E)

    @pl.when(nt > 0)
    def _():
        wa_s[...] = w_raw[slot, 0].astype(BF16)
        wb_s[...] = w_raw[slot, 1].astype(BF16)

    def compute(xt):
        x = xt.astype(BF16)
        hg = jnp.dot(x, wa_s[...], preferred_element_type=F32) + ba_ref[...]
        hl = jnp.dot(x, wb_s[...], preferred_element_type=F32) + bb_ref[...]
        hg = jnp.minimum(hg, SWIGLU_LIMIT)
        hl = jnp.clip(hl, -SWIGLU_LIMIT, SWIGLU_LIMIT)
        return (hg * jax.nn.sigmoid(SWIGLU_ALPHA * hg) * (hl + 1.0)).astype(BF16)

    used = t0_ref[E - 1] + nt_ref[E - 1]
    _group_tile_loop(t0, nt, used, n_tiles, e == E - 1, xs_ref, o_ref, in_buf, out_buf, in_sem, out_sem,
                     pl.multiple_of(n * tn, tn), tm, tn, compute)


def _gmm1(t0, nt, xs, w1, b1, tm, tn):
    rows = xs.shape[0]
    E, D, F2 = w1.shape
    F = F2 // 2
    nb = F // tn
    b1r = b1.reshape(E, 1, F2)
    kern = functools.partial(_gmm1_kernel, tm=tm, tn=tn, n_tiles=rows // tm)
    grid_spec = pltpu.PrefetchScalarGridSpec(
        num_scalar_prefetch=2,
        grid=(nb, E),
        in_specs=[
            pl.BlockSpec((None, 1, tn), lambda n, e, t0, nt: (e, 0, n)),
            pl.BlockSpec((None, 1, tn), lambda n, e, t0, nt: (e, 0, nb + n)),
            pl.BlockSpec(memory_space=pl.ANY),
            pl.BlockSpec(memory_space=pl.ANY),
        ],
        out_specs=pl.BlockSpec(memory_space=pl.ANY),
        scratch_shapes=[
            pltpu.VMEM((2, 2, D, tn), F32), pltpu.SemaphoreType.DMA((2, 2)),
            pltpu.VMEM((D, tn), BF16), pltpu.VMEM((D, tn), BF16),
            pltpu.VMEM((2, tm, D), F32), pltpu.VMEM((2, tm, tn), BF16),
            pltpu.SemaphoreType.DMA((2,)), pltpu.SemaphoreType.DMA((2,)),
        ],
    )
    return pl.pallas_call(
        kern,
        out_shape=jax.ShapeDtypeStruct((rows, F), BF16),
        grid_spec=grid_spec,
        compiler_params=_cparams("arbitrary", "arbitrary"),
        name="gmm1_swiglu",
    )(t0, nt, b1r, b1r, w1, xs)


def _gmm2_kernel(t0_ref, nt_ref, b_ref, w_ref, a_ref, o_ref, w_raw, w_sem, w_s, in_buf, out_buf,
                 in_sem, out_sem, *, tm, tn, n_tiles):
    n = pl.program_id(0)
    e = pl.program_id(1)
    E = pl.num_programs(1)
    t0 = t0_ref[e]
    nt = nt_ref[e]

    @pl.when(nt > 0)
    def _():
        _tile_read(a_ref, in_buf, in_sem, 0, t0, tm).start(priority=TILE_READ_PRIORITY)

    def weight_copies(slot, n_, e_):
        return [pltpu.make_async_copy(w_ref.at[e_, :, pl.ds(pl.multiple_of(n_ * tn, tn), tn)],
                                      w_raw.at[slot], w_sem.at[slot])]

    slot = _stream_expert_weights(weight_copies, pl.num_programs(0) * E)

    @pl.when(nt > 0)
    def _():
        w_s[...] = w_raw[slot].astype(BF16)

    def compute(a):
        return jnp.dot(a, w_s[...], preferred_element_type=F32) + b_ref[...]

    used = t0_ref[E - 1] + nt_ref[E - 1]
    _group_tile_loop(t0, nt, used, n_tiles, e == E - 1, a_ref, o_ref, in_buf, out_buf, in_sem, out_sem,
                     pl.multiple_of(n * tn, tn), tm, tn, compute)


def _gmm2(t0, nt, act, w2, b2, tm, tn):
    rows, F = act.shape
    E, _, D = w2.shape
    b2r = b2.reshape(E, 1, D)
    kern = functools.partial(_gmm2_kernel, tm=tm, tn=tn, n_tiles=rows // tm)
    grid_spec = pltpu.PrefetchScalarGridSpec(
        num_scalar_prefetch=2,
        grid=(D // tn, E),
        in_specs=[
            pl.BlockSpec((None, 1, tn), lambda n, e, t0, nt: (e, 0, n)),
            pl.BlockSpec(memory_space=pl.ANY),
            pl.BlockSpec(memory_space=pl.ANY),
        ],
        out_specs=pl.BlockSpec(memory_space=pl.ANY),
        scratch_shapes=[
            pltpu.VMEM((2, F, tn), F32), pltpu.SemaphoreType.DMA((2,)),
            pltpu.VMEM((F, tn), BF16),
            pltpu.VMEM((2, tm, F), BF16), pltpu.VMEM((2, tm, tn), F32),
            pltpu.SemaphoreType.DMA((2,)), pltpu.SemaphoreType.DMA((2,)),
        ],
    )
    return pl.pallas_call(
        kern,
        out_shape=jax.ShapeDtypeStruct((rows, D), F32),
        grid_spec=grid_spec,
        compiler_params=_cparams("arbitrary", "arbitrary"),
        name="gmm2",
    )(t0, nt, b2r, w2, act)


def _tile_plan(counts, n_tiles, tm):
    ptiles = (counts + tm - 1) // tm
    ends = jnp.cumsum(ptiles)
    starts = ends - ptiles
    t = jnp.arange(n_tiles, dtype=I32)[:, None]
    last_of_group = jnp.any((ptiles[None, :] > 0) & (t == ends[None, :] - 1), axis=1)
    zflag = (last_of_group | (t[:, 0] >= ends[-1])).astype(I32)
    return starts.astype(I32), ptiles.astype(I32), (starts * tm).astype(I32), zflag


def kernel(x, norm_mix_g, w_in, conv_w, conv_b, conv_ln_g, conv_ln_b, w_conv_out, pool_w, pool_scale,
           w_pool_out, w_o, norm_ffn_g, w_router, b_router, w1, b1, w2, b2, norm_final_g):
    B, S, D = x.shape
    T = B * S
    assert w_in.shape[0] == 1, "the combine kernel fuses the final norm: one block only"
    E = w_router.shape[2]
    F = w2.shape[2]
    C = w_conv_out.shape[1]
    h = x.reshape(T, D)
    w_in_bf = w_in[0].astype(BF16)
    g1 = norm_mix_g[0].reshape(1, D)
    u, p, sg = _inproj(h, g1, w_in_bf, C, _tile(T, 1024), _tile(C, 512))
    ts_seq = _tile(S, 512)
    c = _conv_branch(u.reshape(B, S, C), conv_w[0], conv_b[0], conv_ln_g[0], conv_ln_b[0], ts_seq)
    q = _pool_branch(p.reshape(B, S, C), pool_w[0].astype(BF16), pool_scale[0], ts_seq)
    g2 = norm_ffn_g[0].reshape(1, D)
    h1 = _merge(c.reshape(T, C), q.reshape(T, C), sg, h, w_conv_out[0].astype(BF16),
                w_pool_out[0].astype(BF16), w_o[0].astype(BF16), _tile(T, 256))
    wr_pad = jnp.pad(w_router[0], ((0, 0), (0, V7X_LANES - E)))
    idx_t, w_t, rank_t, cnt = _router(h1, g2, wr_pad, b_router[0].reshape(E, 1), E, _tile(T, 512))
    tm = _tile(T, 256)
    n_tiles = (T * TOP_K) // tm + E
    t0, nt, row_start, zflag = _tile_plan(cnt[:, 0].astype(I32), n_tiles, tm)
    pos = _positions(row_start, idx_t, rank_t, _tile(T, 2048))
    xs = _dispatch(zflag, pos, h1, g2, n_tiles, tm, _tile(T, 512))
    act = _gmm1(t0, nt, xs, w1[0], b1[0], tm, _tile(F, 1024))
    y = _gmm2(t0, nt, act, w2[0], b2[0], tm, _tile(D, 2048))
    out = _combine(pos, w_t.T, h1, norm_final_g.reshape(1, D), y, _tile(T, 256))
    return out.reshape(B, S, D)
```

```python
import functools

import jax
import jax.numpy as jnp
from jax import lax
from jax.experimental import pallas as pl
from jax.experimental.pallas import tpu as pltpu

NORM_EPS = 1e-5
POOL_WINDOWS = (2, 4, 8, 16)
TOP_K = 4
SWIGLU_ALPHA = 1.702
SWIGLU_LIMIT = 7.0

V7X_LANES = 128
V7X_SUBLANES = 8
V7X_VMEM_BYTES = 64 * 1024 * 1024
VMEM_LIMIT_BYTES = V7X_VMEM_BYTES - 8 * 1024 * 1024

F32 = jnp.float32
BF16 = jnp.bfloat16
I32 = jnp.int32


def _cparams(*sem):
    return pltpu.CompilerParams(dimension_semantics=sem, vmem_limit_bytes=VMEM_LIMIT_BYTES)


def _tile(n, pref):
    t = min(n, pref)
    while n % t:
        t //= 2
    return t


def _rms(x, g):
    ms = jnp.mean(x * x, axis=-1, keepdims=True)
    return x * lax.rsqrt(ms + NORM_EPS) * g


INPROJ_ROW_CHUNKS = 4


def _inproj_kernel(x_ref, g_ref, wa_ref, wb_ref, u_ref, p_ref, sg_ref, xn_ref, *, nu, npool):
    j = pl.program_id(1)

    @pl.when(j == 0)
    def _():
        xn_ref[...] = _rms(x_ref[...], g_ref[...]).astype(BF16)

    tm = x_ref.shape[0]
    rc = tm // INPROJ_ROW_CHUNKS
    chunks = [slice(c * rc, (c + 1) * rc) for c in range(INPROJ_ROW_CHUNKS)]

    def proj(w_ref, rows):
        return jnp.dot(xn_ref[rows, :], w_ref[...], preferred_element_type=F32)

    @pl.when(j < nu)
    def _():
        for rows in chunks:
            u_ref[rows, :] = (proj(wa_ref, rows) * jax.nn.sigmoid(proj(wb_ref, rows))).astype(u_ref.dtype)

    @pl.when((j >= nu) & (j < nu + npool))
    def _():
        for rows in chunks:
            p_ref[rows, :] = proj(wa_ref, rows)

    @pl.when(j >= nu + npool)
    def _():
        for rows in chunks:
            sg_ref[rows, :] = jax.nn.sigmoid(proj(wa_ref, rows)).astype(sg_ref.dtype)


def _inproj(x2, g, w_bf, C, tm, tn):
    T, D = x2.shape
    nu = C // tn
    npool = C // tn
    nsg = (w_bf.shape[1] - 3 * C) // tn
    kern = functools.partial(_inproj_kernel, nu=nu, npool=npool)
    return pl.pallas_call(
        kern,
        out_shape=(jax.ShapeDtypeStruct((T, C), BF16), jax.ShapeDtypeStruct((T, C), F32),
                   jax.ShapeDtypeStruct((T, nsg * tn), BF16)),
        grid=(T // tm, nu + npool + nsg),
        in_specs=[
            pl.BlockSpec((tm, D), lambda i, j: (i, 0)),
            pl.BlockSpec((1, D), lambda i, j: (0, 0)),
            pl.BlockSpec((D, tn), lambda i, j: (0, jnp.where(j < nu, j, j + nu))),
            pl.BlockSpec((D, tn), lambda i, j: (0, nu + jnp.minimum(j, nu - 1))),
        ],
        out_specs=(
            pl.BlockSpec((tm, tn), lambda i, j: (i, jnp.minimum(j, nu - 1))),
            pl.BlockSpec((tm, tn), lambda i, j: (i, jnp.clip(j - nu, 0, npool - 1))),
            pl.BlockSpec((tm, tn), lambda i, j: (i, jnp.clip(j - nu - npool, 0, nsg - 1))),
        ),
        scratch_shapes=[pltpu.VMEM((tm, D), BF16)],
        compiler_params=_cparams("parallel", "arbitrary"),
        name="inproj",
    )(x2, g, w_bf, w_bf)


CONV_HALO = 32
CONV_ROWS = 64
CONV_LANES = V7X_LANES


def _conv_kernel(cur_ref, halo_ref, w_ref, b_ref, g_ref, beta_ref, o_ref, ext_ref, *, ts, C, KW):
    s = pl.program_id(1)
    halo = halo_ref[...].astype(F32)
    ext_ref[0:CONV_HALO, :] = jnp.where(s > 0, halo, 0.0)
    ext_ref[CONV_HALO:, :] = cur_ref[...].astype(F32)
    off = CONV_HALO - (KW - 1)
    lc = min(C, CONV_LANES)
    sub = V7X_SUBLANES
    region_rows = CONV_ROWS + CONV_HALO

    def body(i, carry):
        r0 = pl.multiple_of(i * CONV_ROWS, CONV_ROWS)
        parts = []
        for c0 in range(0, C, lc):
            region = ext_ref[pl.ds(r0, region_rows), c0:c0 + lc]
            acc = jnp.zeros((CONV_ROWS, lc), F32) + b_ref[:, c0:c0 + lc]
            for r in range(sub):
                span = CONV_ROWS + (sub if r else 0)
                part = None
                for a in range(CONV_HALO // sub + 1):
                    k = sub * a + r - off
                    if 0 <= k < KW:
                        assert sub * a + span <= region_rows
                        term = region[sub * a:sub * a + span, :] * w_ref[k:k + 1, c0:c0 + lc]
                        part = term if part is None else part + term
                if part is not None:
                    acc = acc + part[r:r + CONV_ROWS, :]
            parts.append(acc)
        y = jnp.concatenate(parts, axis=1) if len(parts) > 1 else parts[0]
        mu = jnp.mean(y, axis=-1, keepdims=True)
        d = y - mu
        var = jnp.mean(d * d, axis=-1, keepdims=True)
        yn = d * lax.rsqrt(var + NORM_EPS) * g_ref[...] + beta_ref[...]
        o_ref[pl.ds(r0, CONV_ROWS), :] = (yn * jax.nn.sigmoid(yn)).astype(o_ref.dtype)
        return carry

    lax.fori_loop(0, ts // CONV_ROWS, body, 0)


def _conv_branch(u3, conv_w, conv_b, ln_g, ln_b, ts):
    B, S, C = u3.shape
    KW = conv_w.shape[0]
    assert KW - 1 <= CONV_HALO and ts % CONV_HALO == 0
    hb = ts // CONV_HALO
    kern = functools.partial(_conv_kernel, ts=ts, C=C, KW=KW)
    vec = lambda: pl.BlockSpec((1, C), lambda b, s: (0, 0))
    return pl.pallas_call(
        kern,
        out_shape=jax.ShapeDtypeStruct((B, S, C), BF16),
        grid=(B, S // ts),
        in_specs=[
            pl.BlockSpec((None, ts, C), lambda b, s: (b, s, 0)),
            pl.BlockSpec((None, CONV_HALO, C), lambda b, s: (b, jnp.maximum(s * hb - 1, 0), 0)),
            pl.BlockSpec((KW, C), lambda b, s: (0, 0)),
            vec(), vec(), vec(),
        ],
        out_specs=pl.BlockSpec((None, ts, C), lambda b, s: (b, s, 0)),
        scratch_shapes=[pltpu.VMEM((CONV_HALO + ts, C), F32)],
        compiler_params=_cparams("parallel", "arbitrary"),
        name="conv_branch",
    )(u3, u3, conv_w, conv_b.reshape(1, C), ln_g.reshape(1, C), ln_b.reshape(1, C))


POOL_HALO = 16
POOL_ROWS = 32


def _pool_kernel(cur_ref, halo_ref, pw_ref, ps_ref, o_ref, ext_ref, diff_ref, *, ts, C):
    s = pl.program_id(1)
    G = len(POOL_WINDOWS)
    cg = C // G
    ext_ref[0:POOL_HALO, :] = jnp.where(s > 0, halo_ref[...], 0.0)
    ext_ref[POOL_HALO:, :] = cur_ref[...]

    def body(i, carry):
        r0 = pl.multiple_of(i * POOL_ROWS, POOL_ROWS)
        row = lax.broadcasted_iota(I32, (POOL_ROWS, 1), 0)
        pos1 = (s * ts + r0 + row + 1).astype(F32)
        for gi, w in enumerate(POOL_WINDOWS):
            win = ext_ref[pl.ds(r0, POOL_ROWS + POOL_HALO), gi * cg:(gi + 1) * cg]
            tok = win[POOL_HALO:POOL_HALO + POOL_ROWS, :]
            tot = tok
            for j in range(1, w):
                tot = tot + win[POOL_HALO - j:POOL_HALO - j + POOL_ROWS, :]
            d = tot / jnp.minimum(pos1, float(w)) - tok
            diff_ref[pl.ds(r0, POOL_ROWS), gi * cg:(gi + 1) * cg] = d.astype(BF16)
        return carry

    lax.fori_loop(0, ts // POOL_ROWS, body, 0)
    for gi in range(G):
        y = jnp.dot(diff_ref[:, gi * cg:(gi + 1) * cg], pw_ref[gi], preferred_element_type=F32)
        o_ref[:, gi * cg:(gi + 1) * cg] = (y * ps_ref[:, gi * cg:(gi + 1) * cg]).astype(o_ref.dtype)


def _pool_branch(p3, pool_w_bf, pool_scale, ts):
    B, S, C = p3.shape
    G, cg, _ = pool_w_bf.shape
    assert G == len(POOL_WINDOWS) and max(POOL_WINDOWS) - 1 <= POOL_HALO and ts % POOL_ROWS == 0
    hb = ts // POOL_HALO
    kern = functools.partial(_pool_kernel, ts=ts, C=C)
    return pl.pallas_call(
        kern,
        out_shape=jax.ShapeDtypeStruct((B, S, C), BF16),
        grid=(B, S // ts),
        in_specs=[
            pl.BlockSpec((None, ts, C), lambda b, s: (b, s, 0)),
            pl.BlockSpec((None, POOL_HALO, C), lambda b, s: (b, jnp.maximum(s * hb - 1, 0), 0)),
            pl.BlockSpec((G, cg, cg), lambda b, s: (0, 0, 0)),
            pl.BlockSpec((1, C), lambda b, s: (0, 0)),
        ],
        out_specs=pl.BlockSpec((None, ts, C), lambda b, s: (b, s, 0)),
        scratch_shapes=[pltpu.VMEM((POOL_HALO + ts, C), F32), pltpu.VMEM((ts, C), BF16)],
        compiler_params=_cparams("parallel", "arbitrary"),
        name="pool_branch",
    )(p3, p3, pool_w_bf, pool_scale.reshape(1, C))


def _merge_kernel(c_ref, q_ref, sg_ref, x_ref, wco_ref, wpo_ref, wo_ref, h1_ref):
    D = x_ref.shape[1]
    yc = jnp.dot(c_ref[...], wco_ref[...], preferred_element_type=F32)
    yp = jnp.dot(q_ref[...], wpo_ref[...], preferred_element_type=F32)
    merged = sg_ref[:, :D].astype(F32) * yc + sg_ref[:, D:].astype(F32) * yp
    h1_ref[...] = x_ref[...] + jnp.dot(merged.astype(BF16), wo_ref[...], preferred_element_type=F32)


def _merge(c2, q2, sg, x2, wco, wpo, wo, tm):
    T, D = x2.shape
    C = c2.shape[1]
    full = lambda shape: pl.BlockSpec(shape, lambda i: (0, 0), pipeline_mode=pl.Buffered(1))
    return pl.pallas_call(
        _merge_kernel,
        out_shape=jax.ShapeDtypeStruct((T, D), F32),
        grid=(T // tm,),
        in_specs=[
            pl.BlockSpec((tm, C), lambda i: (i, 0)),
            pl.BlockSpec((tm, C), lambda i: (i, 0)),
            pl.BlockSpec((tm, 2 * D), lambda i: (i, 0)),
            pl.BlockSpec((tm, D), lambda i: (i, 0)),
            full((C, D)), full((C, D)), full((D, D)),
        ],
        out_specs=pl.BlockSpec((tm, D), lambda i: (i, 0)),
        compiler_params=_cparams("parallel"),
        name="merge_outproj",
    )(c2, q2, sg, x2, wco, wpo, wo)


def _router_kernel(h1_ref, g2_ref, wr_ref, br_ref, idx_ref, w_ref, rank_ref, cnt_ref, carry_ref, *, E, ts):
    i = pl.program_id(0)

    @pl.when(i == 0)
    def _():
        carry_ref[...] = jnp.zeros_like(carry_ref)

    xn = _rms(h1_ref[...], g2_ref[...])
    x_hi = xn.astype(BF16)
    x_lo = (xn - x_hi.astype(F32)).astype(BF16)
    wr = wr_ref[...]
    w_hi = wr.astype(BF16)
    w_lo = (wr - w_hi.astype(F32)).astype(BF16)
    lg = (jnp.dot(x_hi, w_hi, preferred_element_type=F32)
          + jnp.dot(x_hi, w_lo, preferred_element_type=F32)
          + jnp.dot(x_lo, w_hi, preferred_element_type=F32))
    vals = lg.T[:E, :] + br_ref[...]
    e_iota = lax.broadcasted_iota(I32, (E, ts), 0)
    tops, idxs, hots = [], [], []
    for _ in range(TOP_K):
        m = jnp.max(vals, axis=0, keepdims=True)
        idx = jnp.min(jnp.where(vals == m, e_iota, E), axis=0, keepdims=True)
        hot = e_iota == idx
        vals = jnp.where(hot, -jnp.inf, vals)
        tops.append(m)
        idxs.append(idx)
        hots.append(hot)
    exps = [jnp.exp(t - tops[0]) for t in tops]
    den = exps[0] + exps[1] + exps[2] + exps[3]
    w_ref[...] = jnp.concatenate([e / den for e in exps], axis=0)
    idx_ref[...] = jnp.concatenate(idxs, axis=0)

    mask = jnp.zeros((E, ts), F32)
    for hot in hots:
        mask = mask + hot.astype(F32)
    before = lax.broadcasted_iota(I32, (ts, ts), 0) < lax.broadcasted_iota(I32, (ts, ts), 1)
    excl = jnp.dot(mask.astype(BF16), before.astype(BF16), preferred_element_type=F32)
    base = excl + carry_ref[:, 0:1]
    ranks = [jnp.sum(jnp.where(hot, base, 0.0), axis=0, keepdims=True) for hot in hots]
    rank_ref[...] = jnp.concatenate(ranks, axis=0).astype(I32)
    carry_ref[...] = carry_ref[...] + jnp.sum(mask, axis=1, keepdims=True)
    cnt_ref[...] = carry_ref[...]


def _router(h1, g2, wr_pad, br_col, E, ts):
    T, D = h1.shape
    kern = functools.partial(_router_kernel, E=E, ts=ts)
    tok = lambda: pl.BlockSpec((TOP_K, ts), lambda i: (0, i))
    return pl.pallas_call(
        kern,
        out_shape=(jax.ShapeDtypeStruct((TOP_K, T), I32), jax.ShapeDtypeStruct((TOP_K, T), F32),
                   jax.ShapeDtypeStruct((TOP_K, T), I32), jax.ShapeDtypeStruct((E, V7X_LANES), F32)),
        grid=(T // ts,),
        in_specs=[
            pl.BlockSpec((ts, D), lambda i: (i, 0)),
            pl.BlockSpec((1, D), lambda i: (0, 0)),
            pl.BlockSpec((D, V7X_LANES), lambda i: (0, 0)),
            pl.BlockSpec((E, 1), lambda i: (0, 0)),
        ],
        out_specs=(tok(), tok(), tok(), pl.BlockSpec((E, V7X_LANES), lambda i: (0, 0))),
        scratch_shapes=[pltpu.VMEM((E, V7X_LANES), F32)],
        compiler_params=_cparams("arbitrary"),
        name="router",
    )(h1, g2, wr_pad, br_col)


def _pos_kernel(start_ref, idx_ref, rank_ref, pos_ref, *, E):
    idx = idx_ref[...]
    base = jnp.zeros_like(idx)
    for e in range(E):
        base = jnp.where(idx == e, start_ref[e], base)
    pos_ref[...] = base + rank_ref[...]


def _positions(row_start, idx_t, rank_t, tl):
    K, T = idx_t.shape
    E = row_start.shape[0]
    tok = lambda: pl.BlockSpec((K, tl), lambda i: (0, i))
    return pl.pallas_call(
        functools.partial(_pos_kernel, E=E),
        out_shape=jax.ShapeDtypeStruct((K, T), I32),
        grid=(T // tl,),
        in_specs=[pl.BlockSpec(memory_space=pltpu.SMEM), tok(), tok()],
        out_specs=tok(),
        compiler_params=_cparams("parallel"),
        name="positions",
    )(row_start, idx_t, rank_t)


ROW_COPY_UNROLL = 8


def _row_copy_scatter(src_ref, dst_ref, sem, r, pos):
    return pltpu.make_async_copy(src_ref.at[pl.ds(r, 1)], dst_ref.at[pl.ds(pos, 1)], sem)


def _tile_fill(zero_ref, dst_ref, sem, t, tm):
    return pltpu.make_async_copy(zero_ref, dst_ref.at[pl.ds(pl.multiple_of(t * tm, tm), tm)], sem)


def _dispatch_kernel(zflag_ref, pos_ref, h1_ref, g2_ref, xs_ref, xn_ref, zero_ref, sem, zsem,
                     *, ts, tm, n_tiles):
    i = pl.program_id(0)
    slot = i & 1
    xn_ref[slot] = _rms(h1_ref[...], g2_ref[...])

    @pl.when(i == 0)
    def _():
        zero_ref[...] = jnp.zeros_like(zero_ref)

        def fill(t, carry):
            @pl.when(zflag_ref[t] == 1)
            def _():
                _tile_fill(zero_ref, xs_ref, zsem, t, tm).start()
            return carry

        lax.fori_loop(0, n_tiles, fill, 0)

        def fill_wait(t, carry):
            @pl.when(zflag_ref[t] == 1)
            def _():
                _tile_fill(zero_ref, xs_ref, zsem, t, tm).wait()
            return carry

        lax.fori_loop(0, n_tiles, fill_wait, 0)

    def issue(r, carry):
        for k in range(TOP_K):
            _row_copy_scatter(xn_ref.at[slot], xs_ref, sem.at[slot], r, pos_ref[0, k * ts + r]).start(
                priority=k % 2)
        return carry

    lax.fori_loop(0, ts, issue, 0, unroll=ROW_COPY_UNROLL)

    def drain_tile(s):
        def drain(r, carry):
            for k in range(TOP_K):
                _row_copy_scatter(xn_ref.at[s], xs_ref, sem.at[s], 0, 0).wait()
            return carry

        lax.fori_loop(0, ts, drain, 0, unroll=ROW_COPY_UNROLL)

    @pl.when(i > 0)
    def _():
        drain_tile(1 - slot)

    @pl.when(i == pl.num_programs(0) - 1)
    def _():
        drain_tile(slot)


def _tile_major(pos, ts):
    K, T = pos.shape
    return pos.reshape(K, T // ts, ts).transpose(1, 0, 2).reshape(T // ts, 1, K * ts)


def _dispatch(zflag, pos, h1, g2, n_tiles, tm, ts):
    T, D = h1.shape
    kern = functools.partial(_dispatch_kernel, ts=ts, tm=tm, n_tiles=n_tiles)
    grid_spec = pltpu.PrefetchScalarGridSpec(
        num_scalar_prefetch=1,
        grid=(T // ts,),
        in_specs=[
            pl.BlockSpec((None, 1, TOP_K * ts), lambda i, zf: (i, 0, 0), memory_space=pltpu.SMEM),
            pl.BlockSpec((ts, D), lambda i, zf: (i, 0)),
            pl.BlockSpec((1, D), lambda i, zf: (0, 0)),
        ],
        out_specs=pl.BlockSpec(memory_space=pl.ANY),
        scratch_shapes=[pltpu.VMEM((2, ts, D), F32), pltpu.VMEM((tm, D), F32),
                        pltpu.SemaphoreType.DMA((2,)), pltpu.SemaphoreType.DMA],
    )
    return pl.pallas_call(
        kern,
        out_shape=jax.ShapeDtypeStruct((n_tiles * tm, D), F32),
        grid_spec=grid_spec,
        compiler_params=_cparams("arbitrary"),
        name="dispatch",
    )(zflag, _tile_major(pos, ts), h1, g2)


def _row_copy_gather(src_ref, dst_ref, sem, slot, k, r, pos):
    return pltpu.make_async_copy(src_ref.at[pl.ds(pos, 1)], dst_ref.at[slot, k, pl.ds(r, 1)], sem.at[slot])


def _combine_kernel(pos_ref, nxt_ref, w_ref, h1_ref, gf_ref, y_ref, o_ref, buf_ref, sem, *, ts):
    i = pl.program_id(0)
    slot = i & 1

    def gather_tile(table_ref, dst_slot):
        def issue(r, carry):
            for k in range(TOP_K):
                _row_copy_gather(y_ref, buf_ref, sem, dst_slot, k, r, table_ref[0, k * ts + r]).start(
                    priority=k % 2)
            return carry

        lax.fori_loop(0, ts, issue, 0, unroll=ROW_COPY_UNROLL)

    @pl.when(i == 0)
    def _():
        gather_tile(pos_ref, 0)

    @pl.when(i + 1 < pl.num_programs(0))
    def _():
        gather_tile(nxt_ref, 1 - slot)

    def drain(r, carry):
        for k in range(TOP_K):
            _row_copy_gather(y_ref, buf_ref, sem, slot, k, 0, 0).wait()
        return carry

    lax.fori_loop(0, ts, drain, 0, unroll=ROW_COPY_UNROLL)
    acc = h1_ref[...]
    for k in range(TOP_K):
        acc = acc + w_ref[:, k:k + 1] * buf_ref[slot, k]
    o_ref[...] = _rms(acc, gf_ref[...])


def _combine(pos, w_tok, h1, gf, y, ts):
    T, D = h1.shape
    n = T // ts
    kern = functools.partial(_combine_kernel, ts=ts)
    pos_tiles = _tile_major(pos, ts)
    return pl.pallas_call(
        kern,
        out_shape=jax.ShapeDtypeStruct((T, D), F32),
        grid=(n,),
        in_specs=[
            pl.BlockSpec((None, 1, TOP_K * ts), lambda i: (i, 0, 0), memory_space=pltpu.SMEM),
            pl.BlockSpec((None, 1, TOP_K * ts), lambda i: (jnp.minimum(i + 1, n - 1), 0, 0),
                         memory_space=pltpu.SMEM),
            pl.BlockSpec((ts, TOP_K), lambda i: (i, 0)),
            pl.BlockSpec((ts, D), lambda i: (i, 0)),
            pl.BlockSpec((1, D), lambda i: (0, 0)),
            pl.BlockSpec(memory_space=pl.ANY),
        ],
        out_specs=pl.BlockSpec((ts, D), lambda i: (i, 0)),
        scratch_shapes=[pltpu.VMEM((2, TOP_K, ts, D), F32), pltpu.SemaphoreType.DMA((2,))],
        compiler_params=_cparams("arbitrary"),
        name="combine",
    )(pos_tiles, pos_tiles, w_tok, h1, gf, y)


TILE_READ_PRIORITY = 0
WEIGHT_READ_PRIORITY = 1


def _tile_read(in_hbm, in_buf, in_sem, slot, t, tm):
    return pltpu.make_async_copy(in_hbm.at[pl.ds(pl.multiple_of(t * tm, tm), tm)], in_buf.at[slot],
                                 in_sem.at[slot])


def _stream_expert_weights(copies, n_steps):
    n = pl.program_id(0)
    e = pl.program_id(1)
    E = pl.num_programs(1)
    s = n * E + e
    slot = s & 1

    @pl.when(s == 0)
    def _():
        for c in copies(0, 0, 0):
            c.start(priority=WEIGHT_READ_PRIORITY)

    @pl.when(s + 1 < n_steps)
    def _():
        wrap = e + 1 == E
        for c in copies(1 - slot, jnp.where(wrap, n + 1, n), jnp.where(wrap, 0, e + 1)):
            c.start(priority=WEIGHT_READ_PRIORITY)

    for c in copies(slot, n, e):
        c.wait()
    return slot


def _group_tile_loop(t0, nt, used, n_tiles, last_group, in_hbm, out_hbm, in_buf, out_buf, in_sem, out_sem,
                     col0, tm, tn, compute):
    def in_copy(slot, t):
        return _tile_read(in_hbm, in_buf, in_sem, slot, t, tm)

    def out_copy(slot, t):
        dst = out_hbm.at[pl.ds(pl.multiple_of(t * tm, tm), tm), pl.ds(col0, tn)]
        return pltpu.make_async_copy(out_buf.at[slot], dst, out_sem.at[slot])

    def body(j, carry):
        slot = j & 1
        in_copy(slot, t0 + j).wait()

        @pl.when(j + 1 < nt)
        def _():
            in_copy(1 - slot, t0 + j + 1).start(priority=TILE_READ_PRIORITY)

        res = compute(in_buf[slot])

        @pl.when(j >= 2)
        def _():
            out_copy(slot, t0 + j - 2).wait()

        out_buf[slot] = res
        out_copy(slot, t0 + j).start()
        return carry

    lax.fori_loop(0, nt, body, 0)

    @pl.when(nt >= 2)
    def _():
        out_copy(nt & 1, t0 + nt - 2).wait()

    @pl.when(nt >= 1)
    def _():
        out_copy((nt - 1) & 1, t0 + nt - 1).wait()

    @pl.when(last_group)
    def _():
        out_buf[0] = jnp.zeros(out_buf.shape[1:], out_buf.dtype)

        def fill(t, carry):
            out_copy(0, t).start()
            out_copy(0, t).wait()
            return carry

        lax.fori_loop(used, n_tiles, fill, 0)


def _gmm1_kernel(t0_ref, nt_ref, ba_ref, bb_ref, w_ref, xs_ref, o_ref,
                 w_raw, w_sem, wa_s, wb_s, in_buf, out_buf, in_sem, out_sem, *, tm, tn, n_tiles):
    n = pl.program_id(0)
    e = pl.program_id(1)
    E = pl.num_programs(1)
    nb = pl.num_programs(0)
    t0 = t0_ref[e]
    nt = nt_ref[e]

    @pl.when(nt > 0)
    def _():
        _tile_read(xs_ref, in_buf, in_sem, 0, t0, tm).start(priority=TILE_READ_PRIORITY)

    def weight_copies(slot, n_, e_):
        return [pltpu.make_async_copy(w_ref.at[e_, :, pl.ds(pl.multiple_of((h * nb + n_) * tn, tn), tn)],
                                      w_raw.at[slot, h], w_sem.at[slot, h]) for h in range(2)]

    slot = _stream_expert_weights(weight_copies, nb * E)

    @pl.when(nt > 0)
    def _():
        wa_s[...] = w_raw[slot, 0].astype(BF16)
        wb_s[...] = w_raw[slot, 1].astype(BF16)

    def compute(xt):
        x = xt.astype(BF16)
        hg = jnp.dot(x, wa_s[...], preferred_element_type=F32) + ba_ref[...]
        hl = jnp.dot(x, wb_s[...], preferred_element_type=F32) + bb_ref[...]
        hg = jnp.minimum(hg, SWIGLU_LIMIT)
        hl = jnp.clip(hl, -SWIGLU_LIMIT, SWIGLU_LIMIT)
        return (hg * jax.nn.sigmoid(SWIGLU_ALPHA * hg) * (hl + 1.0)).astype(BF16)

    used = t0_ref[E - 1] + nt_ref[E - 1]
    _group_tile_loop(t0, nt, used, n_tiles, e == E - 1, xs_ref, o_ref, in_buf, out_buf, in_sem, out_sem,
                     pl.multiple_of(n * tn, tn), tm, tn, compute)


def _gmm1(t0, nt, xs, w1, b1, tm, tn):
    rows = xs.shape[0]
    E, D, F2 = w1.shape
    F = F2 // 2
    nb = F // tn
    b1r = b1.reshape(E, 1, F2)
    kern = functools.partial(_gmm1_kernel, tm=tm, tn=tn, n_tiles=rows // tm)
    grid_spec = pltpu.PrefetchScalarGridSpec(
        num_scalar_prefetch=2,
        grid=(nb, E),
        in_specs=[
            pl.BlockSpec((None, 1, tn), lambda n, e, t0, nt: (e, 0, n)),
            pl.BlockSpec((None, 1, tn), lambda n, e, t0, nt: (e, 0, nb + n)),
            pl.BlockSpec(memory_space=pl.ANY),
            pl.BlockSpec(memory_space=pl.ANY),
        ],
        out_specs=pl.BlockSpec(memory_space=pl.ANY),
        scratch_shapes=[
            pltpu.VMEM((2, 2, D, tn), F32), pltpu.SemaphoreType.DMA((2, 2)),
            pltpu.VMEM((D, tn), BF16), pltpu.VMEM((D, tn), BF16),
            pltpu.VMEM((2, tm, D), F32), pltpu.VMEM((2, tm, tn), BF16),
            pltpu.SemaphoreType.DMA((2,)), pltpu.SemaphoreType.DMA((2,)),
        ],
    )
    return pl.pallas_call(
        kern,
        out_shape=jax.ShapeDtypeStruct((rows, F), BF16),
        grid_spec=grid_spec,
        compiler_params=_cparams("arbitrary", "arbitrary"),
        name="gmm1_swiglu",
    )(t0, nt, b1r, b1r, w1, xs)


def _gmm2_kernel(t0_ref, nt_ref, b_ref, w_ref, a_ref, o_ref, w_raw, w_sem, w_s, in_buf, out_buf,
                 in_sem, out_sem, *, tm, tn, n_tiles):
    n = pl.program_id(0)
    e = pl.program_id(1)
    E = pl.num_programs(1)
    t0 = t0_ref[e]
    nt = nt_ref[e]

    @pl.when(nt > 0)
    def _():
        _tile_read(a_ref, in_buf, in_sem, 0, t0, tm).start(priority=TILE_READ_PRIORITY)

    def weight_copies(slot, n_, e_):
        return [pltpu.make_async_copy(w_ref.at[e_, :, pl.ds(pl.multiple_of(n_ * tn, tn), tn)],
                                      w_raw.at[slot], w_sem.at[slot])]

    slot = _stream_expert_weights(weight_copies, pl.num_programs(0) * E)

    @pl.when(nt > 0)
    def _():
        w_s[...] = w_raw[slot].astype(BF16)

    def compute(a):
        return jnp.dot(a, w_s[...], preferred_element_type=F32) + b_ref[...]

    used = t0_ref[E - 1] + nt_ref[E - 1]
    _group_tile_loop(t0, nt, used, n_tiles, e == E - 1, a_ref, o_ref, in_buf, out_buf, in_sem, out_sem,
                     pl.multiple_of(n * tn, tn), tm, tn, compute)


def _gmm2(t0, nt, act, w2, b2, tm, tn):
    rows, F = act.shape
    E, _, D = w2.shape
    b2r = b2.reshape(E, 1, D)
    kern = functools.partial(_gmm2_kernel, tm=tm, tn=tn, n_tiles=rows // tm)
    grid_spec = pltpu.PrefetchScalarGridSpec(
        num_scalar_prefetch=2,
        grid=(D // tn, E),
        in_specs=[
            pl.BlockSpec((None, 1, tn), lambda n, e, t0, nt: (e, 0, n)),
            pl.BlockSpec(memory_space=pl.ANY),
            pl.BlockSpec(memory_space=pl.ANY),
        ],
        out_specs=pl.BlockSpec(memory_space=pl.ANY),
        scratch_shapes=[
            pltpu.VMEM((2, F, tn), F32), pltpu.SemaphoreType.DMA((2,)),
            pltpu.VMEM((F, tn), BF16),
            pltpu.VMEM((2, tm, F), BF16), pltpu.VMEM((2, tm, tn), F32),
            pltpu.SemaphoreType.DMA((2,)), pltpu.SemaphoreType.DMA((2,)),
        ],
    )
    return pl.pallas_call(
        kern,
        out_shape=jax.ShapeDtypeStruct((rows, D), F32),
        grid_spec=grid_spec,
        compiler_params=_cparams("arbitrary", "arbitrary"),
        name="gmm2",
    )(t0, nt, b2r, w2, act)


def _tile_plan(counts, n_tiles, tm):
    ptiles = (counts + tm - 1) // tm
    ends = jnp.cumsum(ptiles)
    starts = ends - ptiles
    t = jnp.arange(n_tiles, dtype=I32)[:, None]
    last_of_group = jnp.any((ptiles[None, :] > 0) & (t == ends[None, :] - 1), axis=1)
    zflag = (last_of_group | (t[:, 0] >= ends[-1])).astype(I32)
    return starts.astype(I32), ptiles.astype(I32), (starts * tm).astype(I32), zflag


def kernel(x, norm_mix_g, w_in, conv_w, conv_b, conv_ln_g, conv_ln_b, w_conv_out, pool_w, pool_scale,
           w_pool_out, w_o, norm_ffn_g, w_router, b_router, w1, b1, w2, b2, norm_final_g):
    B, S, D = x.shape
    T = B * S
    assert w_in.shape[0] == 1, "the combine kernel fuses the final norm: one block only"
    E = w_router.shape[2]
    F = w2.shape[2]
    C = w_conv_out.shape[1]
    h = x.reshape(T, D)
    w_in_bf = w_in[0].astype(BF16)
    g1 = norm_mix_g[0].reshape(1, D)
    u, p, sg = _inproj(h, g1, w_in_bf, C, _tile(T, 1024), _tile(C, 512))
    ts_seq = _tile(S, 512)
    c = _conv_branch(u.reshape(B, S, C), conv_w[0], conv_b[0], conv_ln_g[0], conv_ln_b[0], ts_seq)
    q = _pool_branch(p.reshape(B, S, C), pool_w[0].astype(BF16), pool_scale[0], ts_seq)
    g2 = norm_ffn_g[0].reshape(1, D)
    h1 = _merge(c.reshape(T, C), q.reshape(T, C), sg, h, w_conv_out[0].astype(BF16),
                w_pool_out[0].astype(BF16), w_o[0].astype(BF16), _tile(T, 256))
    wr_pad = jnp.pad(w_router[0], ((0, 0), (0, V7X_LANES - E)))
    idx_t, w_t, rank_t, cnt = _router(h1, g2, wr_pad, b_router[0].reshape(E, 1), E, _tile(T, 512))
    tm = _tile(T, 256)
    n_tiles = (T * TOP_K) // tm + E
    t0, nt, row_start, zflag = _tile_plan(cnt[:, 0].astype(I32), n_tiles, tm)
    pos = _positions(row_start, idx_t, rank_t, _tile(T, 2048))
    xs = _dispatch(zflag, pos, h1, g2, n_tiles, tm, _tile(T, 512))
    act = _gmm1(t0, nt, xs, w1[0], b1[0], tm, _tile(F, 1024))
    y = _gmm2(t0, nt, act, w2[0], b2[0], tm, _tile(D, 2048))
    out = _combine(pos, w_t.T, h1, norm_final_g.reshape(1, D), y, _tile(T, 256))
    return out.reshape(B, S, D)
```

```python
import functools

import jax
import jax.numpy as jnp
from jax import lax
from jax.experimental import pallas as pl
from jax.experimental.pallas import tpu as pltpu

NORM_EPS = 1e-5
POOL_WINDOWS = (2, 4, 8, 16)
TOP_K = 4
SWIGLU_ALPHA = 1.702
SWIGLU_LIMIT = 7.0

V7X_LANES = 128
V7X_SUBLANES = 8
V7X_VMEM_BYTES = 64 * 1024 * 1024
VMEM_LIMIT_BYTES = V7X_VMEM_BYTES - 8 * 1024 * 1024

F32 = jnp.float32
BF16 = jnp.bfloat16
I32 = jnp.int32


def _cparams(*sem):
    return pltpu.CompilerParams(dimension_semantics=sem, vmem_limit_bytes=VMEM_LIMIT_BYTES)


def _tile(n, pref):
    t = min(n, pref)
    while n % t:
        t //= 2
    return t


def _rms(x, g):
    ms = jnp.mean(x * x, axis=-1, keepdims=True)
    return x * lax.rsqrt(ms + NORM_EPS) * g


INPROJ_ROW_CHUNKS = 4


def _inproj_kernel(x_ref, g_ref, wa_ref, wb_ref, u_ref, p_ref, sg_ref, xn_ref, *, nu, npool):
    j = pl.program_id(1)

    @pl.when(j == 0)
    def _():
        xn_ref[...] = _rms(x_ref[...], g_ref[...]).astype(BF16)

    tm = x_ref.shape[0]
    rc = tm // INPROJ_ROW_CHUNKS
    chunks = [slice(c * rc, (c + 1) * rc) for c in range(INPROJ_ROW_CHUNKS)]

    def proj(w_ref, rows):
        return jnp.dot(xn_ref[rows, :], w_ref[...], preferred_element_type=F32)

    @pl.when(j < nu)
    def _():
        for rows in chunks:
            u_ref[rows, :] = (proj(wa_ref, rows) * jax.nn.sigmoid(proj(wb_ref, rows))).astype(u_ref.dtype)

    @pl.when((j >= nu) & (j < nu + npool))
    def _():
        for rows in chunks:
            p_ref[rows, :] = proj(wa_ref, rows)

    @pl.when(j >= nu + npool)
    def _():
        for rows in chunks:
            sg_ref[rows, :] = jax.nn.sigmoid(proj(wa_ref, rows)).astype(sg_ref.dtype)


def _inproj(x2, g, w_bf, C, tm, tn):
    T, D = x2.shape
    nu = C // tn
    npool = C // tn
    nsg = (w_bf.shape[1] - 3 * C) // tn
    kern = functools.partial(_inproj_kernel, nu=nu, npool=npool)
    return pl.pallas_call(
        kern,
        out_shape=(jax.ShapeDtypeStruct((T, C), BF16), jax.ShapeDtypeStruct((T, C), F32),
                   jax.ShapeDtypeStruct((T, nsg * tn), BF16)),
        grid=(T // tm, nu + npool + nsg),
        in_specs=[
            pl.BlockSpec((tm, D), lambda i, j: (i, 0)),
            pl.BlockSpec((1, D), lambda i, j: (0, 0)),
            pl.BlockSpec((D, tn), lambda i, j: (0, jnp.where(j < nu, j, j + nu))),
            pl.BlockSpec((D, tn), lambda i, j: (0, nu + jnp.minimum(j, nu - 1))),
        ],
        out_specs=(
            pl.BlockSpec((tm, tn), lambda i, j: (i, jnp.minimum(j, nu - 1))),
            pl.BlockSpec((tm, tn), lambda i, j: (i, jnp.clip(j - nu, 0, npool - 1))),
            pl.BlockSpec((tm, tn), lambda i, j: (i, jnp.clip(j - nu - npool, 0, nsg - 1))),
        ),
        scratch_shapes=[pltpu.VMEM((tm, D), BF16)],
        compiler_params=_cparams("parallel", "arbitrary"),
        name="inproj",
    )(x2, g, w_bf, w_bf)


CONV_HALO = 32
CONV_ROWS = 64
CONV_LANES = V7X_LANES


def _conv_kernel(cur_ref, halo_ref, w_ref, b_ref, g_ref, beta_ref, o_ref, ext_ref, *, ts, C, KW):
    s = pl.program_id(1)
    halo = halo_ref[...].astype(F32)
    ext_ref[0:CONV_HALO, :] = jnp.where(s > 0, halo, 0.0)
    ext_ref[CONV_HALO:, :] = cur_ref[...].astype(F32)
    off = CONV_HALO - (KW - 1)
    lc = min(C, CONV_LANES)
    sub = V7X_SUBLANES
    region_rows = CONV_ROWS + CONV_HALO

    def body(i, carry):
        r0 = pl.multiple_of(i * CONV_ROWS, CONV_ROWS)
        parts = []
        for c0 in range(0, C, lc):
            region = ext_ref[pl.ds(r0, region_rows), c0:c0 + lc]
            acc = jnp.zeros((CONV_ROWS, lc), F32) + b_ref[:, c0:c0 + lc]
            for r in range(sub):
                span = CONV_ROWS + (sub if r else 0)
                part = None
                for a in range(CONV_HALO // sub + 1):
                    k = sub * a + r - off
                    if 0 <= k < KW:
                        assert sub * a + span <= region_rows
                        term = region[sub * a:sub * a + span, :] * w_ref[k:k + 1, c0:c0 + lc]
                        part = term if part is None else part + term
                if part is not None:
                    acc = acc + part[r:r + CONV_ROWS, :]
            parts.append(acc)
        y = jnp.concatenate(parts, axis=1) if len(parts) > 1 else parts[0]
        mu = jnp.mean(y, axis=-1, keepdims=True)
        d = y - mu
        var = jnp.mean(d * d, axis=-1, keepdims=True)
        yn = d * lax.rsqrt(var + NORM_EPS) * g_ref[...] + beta_ref[...]
        o_ref[pl.ds(r0, CONV_ROWS), :] = (yn * jax.nn.sigmoid(yn)).astype(o_ref.dtype)
        return carry

    lax.fori_loop(0, ts // CONV_ROWS, body, 0)


def _conv_branch(u3, conv_w, conv_b, ln_g, ln_b, ts):
    B, S, C = u3.shape
    KW = conv_w.shape[0]
    assert KW - 1 <= CONV_HALO and ts % CONV_HALO == 0
    hb = ts // CONV_HALO
    kern = functools.partial(_conv_kernel, ts=ts, C=C, KW=KW)
    vec = lambda: pl.BlockSpec((1, C), lambda b, s: (0, 0))
    return pl.pallas_call(
        kern,
        out_shape=jax.ShapeDtypeStruct((B, S, C), BF16),
        grid=(B, S // ts),
        in_specs=[
            pl.BlockSpec((None, ts, C), lambda b, s: (b, s, 0)),
            pl.BlockSpec((None, CONV_HALO, C), lambda b, s: (b, jnp.maximum(s * hb - 1, 0), 0)),
            pl.BlockSpec((KW, C), lambda b, s: (0, 0)),
            vec(), vec(), vec(),
        ],
        out_specs=pl.BlockSpec((None, ts, C), lambda b, s: (b, s, 0)),
        scratch_shapes=[pltpu.VMEM((CONV_HALO + ts, C), F32)],
        compiler_params=_cparams("parallel", "arbitrary"),
        name="conv_branch",
    )(u3, u3, conv_w, conv_b.reshape(1, C), ln_g.reshape(1, C), ln_b.reshape(1, C))


POOL_HALO = 16
POOL_ROWS = 32


def _pool_kernel(cur_ref, halo_ref, pw_ref, ps_ref, o_ref, ext_ref, diff_ref, *, ts, C):
    s = pl.program_id(1)
    G = len(POOL_WINDOWS)
    cg = C // G
    ext_ref[0:POOL_HALO, :] = jnp.where(s > 0, halo_ref[...], 0.0)
    ext_ref[POOL_HALO:, :] = cur_ref[...]

    def body(i, carry):
        r0 = pl.multiple_of(i * POOL_ROWS, POOL_ROWS)
        row = lax.broadcasted_iota(I32, (POOL_ROWS, 1), 0)
        pos1 = (s * ts + r0 + row + 1).astype(F32)
        for gi, w in enumerate(POOL_WINDOWS):
            win = ext_ref[pl.ds(r0, POOL_ROWS + POOL_HALO), gi * cg:(gi + 1) * cg]
            tok = win[POOL_HALO:POOL_HALO + POOL_ROWS, :]
            tot = tok
            for j in range(1, w):
                tot = tot + win[POOL_HALO - j:POOL_HALO - j + POOL_ROWS, :]
            d = tot / jnp.minimum(pos1, float(w)) - tok
            diff_ref[pl.ds(r0, POOL_ROWS), gi * cg:(gi + 1) * cg] = d.astype(BF16)
        return carry

    lax.fori_loop(0, ts // POOL_ROWS, body, 0)
    for gi in range(G):
        y = jnp.dot(diff_ref[:, gi * cg:(gi + 1) * cg], pw_ref[gi], preferred_element_type=F32)
        o_ref[:, gi * cg:(gi + 1) * cg] = (y * ps_ref[:, gi * cg:(gi + 1) * cg]).astype(o_ref.dtype)


def _pool_branch(p3, pool_w_bf, pool_scale, ts):
    B, S, C = p3.shape
    G, cg, _ = pool_w_bf.shape
    assert G == len(POOL_WINDOWS) and max(POOL_WINDOWS) - 1 <= POOL_HALO and ts % POOL_ROWS == 0
    hb = ts // POOL_HALO
    kern = functools.partial(_pool_kernel, ts=ts, C=C)
    return pl.pallas_call(
        kern,
        out_shape=jax.ShapeDtypeStruct((B, S, C), BF16),
        grid=(B, S // ts),
        in_specs=[
            pl.BlockSpec((None, ts, C), lambda b, s: (b, s, 0)),
            pl.BlockSpec((None, POOL_HALO, C), lambda b, s: (b, jnp.maximum(s * hb - 1, 0), 0)),
            pl.BlockSpec((G, cg, cg), lambda b, s: (0, 0, 0)),
            pl.BlockSpec((1, C), lambda b, s: (0, 0)),
        ],
        out_specs=pl.BlockSpec((None, ts, C), lambda b, s: (b, s, 0)),
        scratch_shapes=[pltpu.VMEM((POOL_HALO + ts, C), F32), pltpu.VMEM((ts, C), BF16)],
        compiler_params=_cparams("parallel", "arbitrary"),
        name="pool_branch",
    )(p3, p3, pool_w_bf, pool_scale.reshape(1, C))


def _merge_kernel(c_ref, q_ref, sg_ref, x_ref, wco_ref, wpo_ref, wo_ref, h1_ref):
    D = x_ref.shape[1]
    yc = jnp.dot(c_ref[...], wco_ref[...], preferred_element_type=F32)
    yp = jnp.dot(q_ref[...], wpo_ref[...], preferred_element_type=F32)
    merged = sg_ref[:, :D].astype(F32) * yc + sg_ref[:, D:].astype(F32) * yp
    h1_ref[...] = x_ref[...] + jnp.dot(merged.astype(BF16), wo_ref[...], preferred_element_type=F32)


def _merge(c2, q2, sg, x2, wco, wpo, wo, tm):
    T, D = x2.shape
    C = c2.shape[1]
    full = lambda shape: pl.BlockSpec(shape, lambda i: (0, 0), pipeline_mode=pl.Buffered(1))
    return pl.pallas_call(
        _merge_kernel,
        out_shape=jax.ShapeDtypeStruct((T, D), F32),
        grid=(T // tm,),
        in_specs=[
            pl.BlockSpec((tm, C), lambda i: (i, 0)),
            pl.BlockSpec((tm, C), lambda i: (i, 0)),
            pl.BlockSpec((tm, 2 * D), lambda i: (i, 0)),
            pl.BlockSpec((tm, D), lambda i: (i, 0)),
            full((C, D)), full((C, D)), full((D, D)),
        ],
        out_specs=pl.BlockSpec((tm, D), lambda i: (i, 0)),
        compiler_params=_cparams("parallel"),
        name="merge_outproj",
    )(c2, q2, sg, x2, wco, wpo, wo)


def _router_kernel(h1_ref, g2_ref, wr_ref, br_ref, idx_ref, w_ref, rank_ref, cnt_ref, carry_ref, *, E, ts):
    i = pl.program_id(0)

    @pl.when(i == 0)
    def _():
        carry_ref[...] = jnp.zeros_like(carry_ref)

    xn = _rms(h1_ref[...], g2_ref[...])
    x_hi = xn.astype(BF16)
    x_lo = (xn - x_hi.astype(F32)).astype(BF16)
    wr = wr_ref[...]
    w_hi = wr.astype(BF16)
    w_lo = (wr - w_hi.astype(F32)).astype(BF16)
    lg = (jnp.dot(x_hi, w_hi, preferred_element_type=F32)
          + jnp.dot(x_hi, w_lo, preferred_element_type=F32)
          + jnp.dot(x_lo, w_hi, preferred_element_type=F32))
    vals = lg.T[:E, :] + br_ref[...]
    e_iota = lax.broadcasted_iota(I32, (E, ts), 0)
    tops, idxs, hots = [], [], []
    for _ in range(TOP_K):
        m = jnp.max(vals, axis=0, keepdims=True)
        idx = jnp.min(jnp.where(vals == m, e_iota, E), axis=0, keepdims=True)
        hot = e_iota == idx
        vals = jnp.where(hot, -jnp.inf, vals)
        tops.append(m)
        idxs.append(idx)
        hots.append(hot)
    exps = [jnp.exp(t - tops[0]) for t in tops]
    den = exps[0] + exps[1] + exps[2] + exps[3]
    w_ref[...] = jnp.concatenate([e / den for e in exps], axis=0)
    idx_ref[...] = jnp.concatenate(idxs, axis=0)

    mask = jnp.zeros((E, ts), F32)
    for hot in hots:
        mask = mask + hot.astype(F32)
    before = lax.broadcasted_iota(I32, (ts, ts), 0) < lax.broadcasted_iota(I32, (ts, ts), 1)
    excl = jnp.dot(mask.astype(BF16), before.astype(BF16), preferred_element_type=F32)
    base = excl + carry_ref[:, 0:1]
    ranks = [jnp.sum(jnp.where(hot, base, 0.0), axis=0, keepdims=True) for hot in hots]
    rank_ref[...] = jnp.concatenate(ranks, axis=0).astype(I32)
    carry_ref[...] = carry_ref[...] + jnp.sum(mask, axis=1, keepdims=True)
    cnt_ref[...] = carry_ref[...]


def _router(h1, g2, wr_pad, br_col, E, ts):
    T, D = h1.shape
    kern = functools.partial(_router_kernel, E=E, ts=ts)
    tok = lambda: pl.BlockSpec((TOP_K, ts), lambda i: (0, i))
    return pl.pallas_call(
        kern,
        out_shape=(jax.ShapeDtypeStruct((TOP_K, T), I32), jax.ShapeDtypeStruct((TOP_K, T), F32),
                   jax.ShapeDtypeStruct((TOP_K, T), I32), jax.ShapeDtypeStruct((E, V7X_LANES), F32)),
        grid=(T // ts,),
        in_specs=[
            pl.BlockSpec((ts, D), lambda i: (i, 0)),
            pl.BlockSpec((1, D), lambda i: (0, 0)),
            pl.BlockSpec((D, V7X_LANES), lambda i: (0, 0)),
            pl.BlockSpec((E, 1), lambda i: (0, 0)),
        ],
        out_specs=(tok(), tok(), tok(), pl.BlockSpec((E, V7X_LANES), lambda i: (0, 0))),
        scratch_shapes=[pltpu.VMEM((E, V7X_LANES), F32)],
        compiler_params=_cparams("arbitrary"),
        name="router",
    )(h1, g2, wr_pad, br_col)


def _pos_kernel(start_ref, idx_ref, rank_ref, pos_ref, *, E):
    idx = idx_ref[...]
    base = jnp.zeros_like(idx)
    for e in range(E):
        base = jnp.where(idx == e, start_ref[e], base)
    pos_ref[...] = base + rank_ref[...]


def _positions(row_start, idx_t, rank_t, tl):
    K, T = idx_t.shape
    E = row_start.shape[0]
    tok = lambda: pl.BlockSpec((K, tl), lambda i: (0, i))
    return pl.pallas_call(
        functools.partial(_pos_kernel, E=E),
        out_shape=jax.ShapeDtypeStruct((K, T), I32),
        grid=(T // tl,),
        in_specs=[pl.BlockSpec(memory_space=pltpu.SMEM), tok(), tok()],
        out_specs=tok(),
        compiler_params=_cparams("parallel"),
        name="positions",
    )(row_start, idx_t, rank_t)


ROW_COPY_UNROLL = 8


def _row_copy_scatter(src_ref, dst_ref, sem, r, pos):
    return pltpu.make_async_copy(src_ref.at[pl.ds(r, 1)], dst_ref.at[pl.ds(pos, 1)], sem)


def _tile_fill(zero_ref, dst_ref, sem, t, tm):
    return pltpu.make_async_copy(zero_ref, dst_ref.at[pl.ds(pl.multiple_of(t * tm, tm), tm)], sem)


def _dispatch_kernel(zflag_ref, pos_ref, h1_ref, g2_ref, xs_ref, xn_ref, zero_ref, sem, zsem,
                     *, ts, tm, n_tiles):
    i = pl.program_id(0)
    slot = i & 1
    xn_ref[slot] = _rms(h1_ref[...], g2_ref[...])

    @pl.when(i == 0)
    def _():
        zero_ref[...] = jnp.zeros_like(zero_ref)

        def fill(t, carry):
            @pl.when(zflag_ref[t] == 1)
            def _():
                _tile_fill(zero_ref, xs_ref, zsem, t, tm).start()
            return carry

        lax.fori_loop(0, n_tiles, fill, 0)

        def fill_wait(t, carry):
            @pl.when(zflag_ref[t] == 1)
            def _():
                _tile_fill(zero_ref, xs_ref, zsem, t, tm).wait()
            return carry

        lax.fori_loop(0, n_tiles, fill_wait, 0)

    def issue(r, carry):
        for k in range(TOP_K):
            _row_copy_scatter(xn_ref.at[slot], xs_ref, sem.at[slot], r, pos_ref[0, k * ts + r]).start(
                priority=k % 2)
        return carry

    lax.fori_loop(0, ts, issue, 0, unroll=ROW_COPY_UNROLL)

    def drain_tile(s):
        def drain(r, carry):
            for k in range(TOP_K):
                _row_copy_scatter(xn_ref.at[s], xs_ref, sem.at[s], 0, 0).wait()
            return carry

        lax.fori_loop(0, ts, drain, 0, unroll=ROW_COPY_UNROLL)

    @pl.when(i > 0)
    def _():
        drain_tile(1 - slot)

    @pl.when(i == pl.num_programs(0) - 1)
    def _():
        drain_tile(slot)


def _tile_major(pos, ts):
    K, T = pos.shape
    return pos.reshape(K, T // ts, ts).transpose(1, 0, 2).reshape(T // ts, 1, K * ts)


def _dispatch(zflag, pos, h1, g2, n_tiles, tm, ts):
    T, D = h1.shape
    kern = functools.partial(_dispatch_kernel, ts=ts, tm=tm, n_tiles=n_tiles)
    grid_spec = pltpu.PrefetchScalarGridSpec(
        num_scalar_prefetch=1,
        grid=(T // ts,),
        in_specs=[
            pl.BlockSpec((None, 1, TOP_K * ts), lambda i, zf: (i, 0, 0), memory_space=pltpu.SMEM),
            pl.BlockSpec((ts, D), lambda i, zf: (i, 0)),
            pl.BlockSpec((1, D), lambda i, zf: (0, 0)),
        ],
        out_specs=pl.BlockSpec(memory_space=pl.ANY),
        scratch_shapes=[pltpu.VMEM((2, ts, D), F32), pltpu.VMEM((tm, D), F32),
                        pltpu.SemaphoreType.DMA((2,)), pltpu.SemaphoreType.DMA],
    )
    return pl.pallas_call(
        kern,
        out_shape=jax.ShapeDtypeStruct((n_tiles * tm, D), F32),
        grid_spec=grid_spec,
        compiler_params=_cparams("arbitrary"),
        name="dispatch",
    )(zflag, _tile_major(pos, ts), h1, g2)


def _row_copy_gather(src_ref, dst_ref, sem, slot, k, r, pos):
    return pltpu.make_async_copy(src_ref.at[pl.ds(pos, 1)], dst_ref.at[slot, k, pl.ds(r, 1)], sem.at[slot])


def _combine_kernel(pos_ref, nxt_ref, w_ref, h1_ref, gf_ref, y_ref, o_ref, buf_ref, sem, *, ts):
    i = pl.program_id(0)
    slot = i & 1

    def gather_tile(table_ref, dst_slot):
        def issue(r, carry):
            for k in range(TOP_K):
                _row_copy_gather(y_ref, buf_ref, sem, dst_slot, k, r, table_ref[0, k * ts + r]).start(
                    priority=k % 2)
            return carry

        lax.fori_loop(0, ts, issue, 0, unroll=ROW_COPY_UNROLL)

    @pl.when(i == 0)
    def _():
        gather_tile(pos_ref, 0)

    @pl.when(i + 1 < pl.num_programs(0))
    def _():
        gather_tile(nxt_ref, 1 - slot)

    def drain(r, carry):
        for k in range(TOP_K):
            _row_copy_gather(y_ref, buf_ref, sem, slot, k, 0, 0).wait()
        return carry

    lax.fori_loop(0, ts, drain, 0, unroll=ROW_COPY_UNROLL)
    acc = h1_ref[...]
    for k in range(TOP_K):
        acc = acc + w_ref[:, k:k + 1] * buf_ref[slot, k]
    o_ref[...] = _rms(acc, gf_ref[...])


def _combine(pos, w_tok, h1, gf, y, ts):
    T, D = h1.shape
    n = T // ts
    kern = functools.partial(_combine_kernel, ts=ts)
    pos_tiles = _tile_major(pos, ts)
    return pl.pallas_call(
        kern,
        out_shape=jax.ShapeDtypeStruct((T, D), F32),
        grid=(n,),
        in_specs=[
            pl.BlockSpec((None, 1, TOP_K * ts), lambda i: (i, 0, 0), memory_space=pltpu.SMEM),
            pl.BlockSpec((None, 1, TOP_K * ts), lambda i: (jnp.minimum(i + 1, n - 1), 0, 0),
                         memory_space=pltpu.SMEM),
            pl.BlockSpec((ts, TOP_K), lambda i: (i, 0)),
            pl.BlockSpec((ts, D), lambda i: (i, 0)),
            pl.BlockSpec((1, D), lambda i: (0, 0)),
            pl.BlockSpec(memory_space=pl.ANY),
        ],
        out_specs=pl.BlockSpec((ts, D), lambda i: (i, 0)),
        scratch_shapes=[pltpu.VMEM((2, TOP_K, ts, D), F32), pltpu.SemaphoreType.DMA((2,))],
        compiler_params=_cparams("arbitrary"),
        name="combine",
    )(pos_tiles, pos_tiles, w_tok, h1, gf, y)


TILE_READ_PRIORITY = 0
WEIGHT_READ_PRIORITY = 1


def _tile_read(in_hbm, in_buf, in_sem, slot, t, tm):
    return pltpu.make_async_copy(in_hbm.at[pl.ds(pl.multiple_of(t * tm, tm), tm)], in_buf.at[slot],
                                 in_sem.at[slot])


def _stream_expert_weights(copies, n_steps):
    n = pl.program_id(0)
    e = pl.program_id(1)
    E = pl.num_programs(1)
    s = n * E + e
    slot = s & 1

    @pl.when(s == 0)
    def _():
        for c in copies(0, 0, 0):
            c.start(priority=WEIGHT_READ_PRIORITY)

    @pl.when(s + 1 < n_steps)
    def _():
        wrap = e + 1 == E
        for c in copies(1 - slot, jnp.where(wrap, n + 1, n), jnp.where(wrap, 0, e + 1)):
            c.start(priority=WEIGHT_READ_PRIORITY)

    for c in copies(slot, n, e):
        c.wait()
    return slot


def _group_tile_loop(t0, nt, used, n_tiles, last_group, in_hbm, out_hbm, in_buf, out_buf, in_sem, out_sem,
                     col0, tm, tn, compute):
    def in_copy(slot, t):
        return _tile_read(in_hbm, in_buf, in_sem, slot, t, tm)

    def out_copy(slot, t):
        dst = out_hbm.at[pl.ds(pl.multiple_of(t * tm, tm), tm), pl.ds(col0, tn)]
        return pltpu.make_async_copy(out_buf.at[slot], dst, out_sem.at[slot])

    def body(j, carry):
        slot = j & 1
        in_copy(slot, t0 + j).wait()

        @pl.when(j + 1 < nt)
        def _():
            in_copy(1 - slot, t0 + j + 1).start(priority=TILE_READ_PRIORITY)

        res = compute(in_buf[slot])

        @pl.when(j >= 2)
        def _():
            out_copy(slot, t0 + j - 2).wait()

        out_buf[slot] = res
        out_copy(slot, t0 + j).start()
        return carry

    lax.fori_loop(0, nt, body, 0)

    @pl.when(nt >= 2)
    def _():
        out_copy(nt & 1, t0 + nt - 2).wait()

    @pl.when(nt >= 1)
    def _():
        out_copy((nt - 1) & 1, t0 + nt - 1).wait()

    @pl.when(last_group)
    def _():
        out_buf[0] = jnp.zeros(out_buf.shape[1:], out_buf.dtype)

        def fill(t, carry):
            out_copy(0, t).start()
            out_copy(0, t).wait()
            return carry

        lax.fori_loop(used, n_tiles, fill, 0)


def _gmm1_kernel(t0_ref, nt_ref, ba_ref, bb_ref, w_ref, xs_ref, o_ref,
                 w_raw, w_sem, wa_s, wb_s, in_buf, out_buf, in_sem, out_sem, *, tm, tn, n_tiles):
    n = pl.program_id(0)
    e = pl.program_id(1)
    E = pl.num_programs(1)
    nb = pl.num_programs(0)
    t0 = t0_ref[e]
    nt = nt_ref[e]

    @pl.when(nt > 0)
    def _():
        _tile_read(xs_ref, in_buf, in_sem, 0, t0, tm).start(priority=TILE_READ_PRIORITY)

    def weight_copies(slot, n_, e_):
        return [pltpu.make_async_copy(w_ref.at[e_, :, pl.ds(pl.multiple_of((h * nb + n_) * tn, tn), tn)],
                                      w_raw.at[slot, h], w_sem.at[slot, h]) for h in range(2)]

    slot = _stream_expert_weights(weight_copies, nb * E)

    @pl.when(nt > 0)
    def _():
        wa_s[...] = w_raw[slot, 0].astype(BF16)
        wb_s[...] = w_raw[slot, 1].astype(BF16)

    def compute(xt):
        x = xt.astype(BF16)
        hg = jnp.dot(x, wa_s[...], preferred_element_type=F32) + ba_ref[...]
        hl = jnp.dot(x, wb_s[...], preferred_element_type=F32) + bb_ref[...]
        hg = jnp.minimum(hg, SWIGLU_LIMIT)
        hl = jnp.clip(hl, -SWIGLU_LIMIT, SWIGLU_LIMIT)
        return (hg * jax.nn.sigmoid(SWIGLU_ALPHA * hg) * (hl + 1.0)).astype(BF16)

    used = t0_ref[E - 1] + nt_ref[E - 1]
    _group_tile_loop(t0, nt, used, n_tiles, e == E - 1, xs_ref, o_ref, in_buf, out_buf, in_sem, out_sem,
                     pl.multiple_of(n * tn, tn), tm, tn, compute)


def _gmm1(t0, nt, xs, w1, b1, tm, tn):
    rows = xs.shape[0]
    E, D, F2 = w1.shape
    F = F2 // 2
    nb = F // tn
    b1r = b1.reshape(E, 1, F2)
    kern = functools.partial(_gmm1_kernel, tm=tm, tn=tn, n_tiles=rows // tm)
    grid_spec = pltpu.PrefetchScalarGridSpec(
        num_scalar_prefetch=2,
        grid=(nb, E),
        in_specs=[
            pl.BlockSpec((None, 1, tn), lambda n, e, t0, nt: (e, 0, n)),
            pl.BlockSpec((None, 1, tn), lambda n, e, t0, nt: (e, 0, nb + n)),
            pl.BlockSpec(memory_space=pl.ANY),
            pl.BlockSpec(memory_space=pl.ANY),
        ],
        out_specs=pl.BlockSpec(memory_space=pl.ANY),
        scratch_shapes=[
            pltpu.VMEM((2, 2, D, tn), F32), pltpu.SemaphoreType.DMA((2, 2)),
            pltpu.VMEM((D, tn), BF16), pltpu.VMEM((D, tn), BF16),
            pltpu.VMEM((2, tm, D), F32), pltpu.VMEM((2, tm, tn), BF16),
            pltpu.SemaphoreType.DMA((2,)), pltpu.SemaphoreType.DMA((2,)),
        ],
    )
    return pl.pallas_call(
        kern,
        out_shape=jax.ShapeDtypeStruct((rows, F), BF16),
        grid_spec=grid_spec,
        compiler_params=_cparams("arbitrary", "arbitrary"),
        name="gmm1_swiglu",
    )(t0, nt, b1r, b1r, w1, xs)


def _gmm2_kernel(t0_ref, nt_ref, b_ref, w_ref, a_ref, o_ref, w_raw, w_sem, w_s, in_buf, out_buf,
                 in_sem, out_sem, *, tm, tn, n_tiles):
    n = pl.program_id(0)
    e = pl.program_id(1)
    E = pl.num_programs(1)
    t0 = t0_ref[e]
    nt = nt_ref[e]

    @pl.when(nt > 0)
    def _():
        _tile_read(a_ref, in_buf, in_sem, 0, t0, tm).start(priority=TILE_READ_PRIORITY)

    def weight_copies(slot, n_, e_):
        return [pltpu.make_async_copy(w_ref.at[e_, :, pl.ds(pl.multiple_of(n_ * tn, tn), tn)],
                                      w_raw.at[slot], w_sem.at[slot])]

    slot = _stream_expert_weights(weight_copies, pl.num_programs(0) * E)

    @pl.when(nt > 0)
    def _():
        w_s[...] = w_raw[slot].astype(BF16)

    def compute(a):
        return jnp.dot(a, w_s[...], preferred_element_type=F32) + b_ref[...]

    used = t0_ref[E - 1] + nt_ref[E - 1]
    _group_tile_loop(t0, nt, used, n_tiles, e == E - 1, a_ref, o_ref, in_buf, out_buf, in_sem, out_sem,
                     pl.multiple_of(n * tn, tn), tm, tn, compute)


def _gmm2(t0, nt, act, w2, b2, tm, tn):
    rows, F = act.shape
    E, _, D = w2.shape
    b2r = b2.reshape(E, 1, D)
    kern = functools.partial(_gmm2_kernel, tm=tm, tn=tn, n_tiles=rows // tm)
    grid_spec = pltpu.PrefetchScalarGridSpec(
        num_scalar_prefetch=2,
        grid=(D // tn, E),
        in_specs=[
            pl.BlockSpec((None, 1, tn), lambda n, e, t0, nt: (e, 0, n)),
            pl.BlockSpec(memory_space=pl.ANY),
            pl.BlockSpec(memory_space=pl.ANY),
        ],
        out_specs=pl.BlockSpec(memory_space=pl.ANY),
        scratch_shapes=[
            pltpu.VMEM((2, F, tn), F32), pltpu.SemaphoreType.DMA((2,)),
            pltpu.VMEM((F, tn), BF16),
            pltpu.VMEM((2, tm, F), BF16), pltpu.VMEM((2, tm, tn), F32),
            pltpu.SemaphoreType.DMA((2,)), pltpu.SemaphoreType.DMA((2,)),
        ],
    )
    return pl.pallas_call(
        kern,
        out_shape=jax.ShapeDtypeStruct((rows, D), F32),
        grid_spec=grid_spec,
        compiler_params=_cparams("arbitrary", "arbitrary"),
        name="gmm2",
    )(t0, nt, b2r, w2, act)


def _tile_plan(counts, n_tiles, tm):
    ptiles = (counts + tm - 1) // tm
    ends = jnp.cumsum(ptiles)
    starts = ends - ptiles
    t = jnp.arange(n_tiles, dtype=I32)[:, None]
    last_of_group = jnp.any((ptiles[None, :] > 0) & (t == ends[None, :] - 1), axis=1)
    zflag = (last_of_group | (t[:, 0] >= ends[-1])).astype(I32)
    return starts.astype(I32), ptiles.astype(I32), (starts * tm).astype(I32), zflag


def kernel(x, norm_mix_g, w_in, conv_w, conv_b, conv_ln_g, conv_ln_b, w_conv_out, pool_w, pool_scale,
           w_pool_out, w_o, norm_ffn_g, w_router, b_router, w1, b1, w2, b2, norm_final_g):
    B, S, D = x.shape
    T = B * S
    assert w_in.shape[0] == 1, "the combine kernel fuses the final norm: one block only"
    E = w_router.shape[2]
    F = w2.shape[2]
    C = w_conv_out.shape[1]
    h = x.reshape(T, D)
    w_in_bf = w_in[0].astype(BF16)
    g1 = norm_mix_g[0].reshape(1, D)
    u, p, sg = _inproj(h, g1, w_in_bf, C, _tile(T, 1024), _tile(C, 512))
    ts_seq = _tile(S, 512)
    c = _conv_branch(u.reshape(B, S, C), conv_w[0], conv_b[0], conv_ln_g[0], conv_ln_b[0], ts_seq)
    q = _pool_branch(p.reshape(B, S, C), pool_w[0].astype(BF16), pool_scale[0], ts_seq)
    g2 = norm_ffn_g[0].reshape(1, D)
    h1 = _merge(c.reshape(T, C), q.reshape(T, C), sg, h, w_conv_out[0].astype(BF16),
                w_pool_out[0].astype(BF16), w_o[0].astype(BF16), _tile(T, 256))
    wr_pad = jnp.pad(w_router[0], ((0, 0), (0, V7X_LANES - E)))
    idx_t, w_t, rank_t, cnt = _router(h1, g2, wr_pad, b_router[0].reshape(E, 1), E, _tile(T, 512))
    tm = _tile(T, 256)
    n_tiles = (T * TOP_K) // tm + E
    t0, nt, row_start, zflag = _tile_plan(cnt[:, 0].astype(I32), n_tiles, tm)
    pos = _positions(row_start, idx_t, rank_t, _tile(T, 2048))
    xs = _dispatch(zflag, pos, h1, g2, n_tiles, tm, _tile(T, 1024))
    act = _gmm1(t0, nt, xs, w1[0], b1[0], tm, _tile(F, 1024))
    y = _gmm2(t0, nt, act, w2[0], b2[0], tm, _tile(D, 2048))
    out = _combine(pos, w_t.T, h1, norm_final_g.reshape(1, D), y, _tile(T, 512))
    return out.reshape(B, S, D)
```
